```python
import math
import jax
import jax.numpy as jnp
from jax import lax
import numpy as np

D_MODEL = 2048
BATCH = 1
SEQ = 8192
DEPTH = 2
DEC_BATCH = 128
DEC_SEQ = 4
PAST_LEN = 16384
PAGE_SIZE = 128

CONV_DIM = 1024
CONV_K = 3
NSA_HEADS = 8
NSA_DK = 128
CMP_BLOCK = 64
N_SELECT = 16
WINDOW = 512
PHI_HID = 256
MLA_HEADS = 8
Q_LORA = 512
KV_LORA = 128
NOPE_DIM = 128
ROPE_DIM = 32
V_DIM = 128
ROPE_THETA = 10000.0
MLA_DC = KV_LORA + ROPE_DIM
N_BUCKETS = 32
MAX_DIST = 128

RMS_EPS = 1e-6
Q_BLOCK = 128
NEG = -1e30
FORCE_SCORE = 1e4
N_BRANCH = 3
NSA_WIDTH = NSA_HEADS * NSA_DK
MLA_WIDTH = MLA_HEADS * V_DIM
MIX_WIDTH = CONV_DIM + NSA_WIDTH + MLA_WIDTH
NSA_SCALE = NSA_DK ** -0.5
MLA_SCALE = (NOPE_DIM + ROPE_DIM) ** -0.5

SPLITS = (
    ('conv_u', CONV_DIM), ('conv_b', CONV_DIM), ('conv_c', CONV_DIM), ('conv_z', CONV_DIM),
    ('nsa_q', NSA_WIDTH), ('nsa_kv', 6 * NSA_DK), ('nsa_g', 3 * NSA_HEADS), ('nsa_z', NSA_WIDTH),
    ('mla_cq', Q_LORA), ('mla_ckv', KV_LORA), ('mla_kr', ROPE_DIM), ('mla_z', MLA_WIDTH),
    ('merge_g', N_BRANCH * D_MODEL),
)
IN_COLS = sum(n for _, n in SPLITS)

kernel_name = 'hybrid_conv_nsa_mla_decode_step'


def split_cols(p):
    out, off = {}, 0
    for name, n in SPLITS:
        out[name] = p[..., off:off + n]
        off += n
    return out


def rmsnorm(x, g):
    xf = x.astype(jnp.float32)
    y = xf * lax.rsqrt(jnp.mean(xf * xf, axis=-1, keepdims=True) + RMS_EPS)
    return (y * g.astype(jnp.float32)).astype(x.dtype)


def rope_angles(pos):
    inv = ROPE_THETA ** (-jnp.arange(0, ROPE_DIM, 2, dtype=jnp.float32) / ROPE_DIM)
    return pos.astype(jnp.float32)[:, None] * inv


def rope(x, ang):
    half = ROPE_DIM // 2
    xf = x.astype(jnp.float32)
    x1, x2 = xf[..., :half], xf[..., half:]
    c, s = jnp.cos(ang), jnp.sin(ang)
    return jnp.concatenate([x1 * c - x2 * s, x2 * c + x1 * s], axis=-1).astype(x.dtype)


def t5_bucket(dist):
    n = jnp.maximum(dist, 0)
    max_exact = N_BUCKETS // 2
    nf = jnp.maximum(n, 1).astype(jnp.float32)
    large = max_exact + (jnp.log(nf / max_exact) / math.log(MAX_DIST / max_exact)
                         * (N_BUCKETS - max_exact)).astype(jnp.int32)
    return jnp.where(n < max_exact, n, jnp.minimum(large, N_BUCKETS - 1))


def t5_bias(table, dist):
    return jnp.moveaxis(table[t5_bucket(dist)], -1, -3).astype(jnp.float32)


def gather_rows(rows, idx):
    return jax.vmap(lambda r, i: r[i])(rows, idx)


def mqa_attend(q, k, v, mask, bias, scale):
    logits = jnp.einsum('...qhd,...ld->...hql', q, k).astype(jnp.float32) * scale
    if bias is not None:
        logits = logits + bias
    m = mask[..., None, :, :]
    logits = jnp.where(m, logits, NEG)
    p = jax.nn.softmax(logits, axis=-1) * m
    out = jnp.einsum('...hql,...ld->...qhd', p.astype(v.dtype), v)
    return out, p


def causal_conv(vp, w):
    t = vp.shape[1] - (CONV_K - 1)
    y = w[0] * vp[:, 0:t]
    for j in range(1, CONV_K):
        y = y + w[j] * vp[:, j:j + t]
    return y


def compress_kv(rows, pos, w1, w2):
    b, t = rows.shape[:2]
    nb = t // CMP_BLOCK
    blk = rows.reshape(b, nb, CMP_BLOCK, 2, NSA_DK)
    hid = jnp.einsum('bncjd,jcdh->bnjh', blk, w1) + jnp.einsum('jcd,jcdh->jh', pos, w1)
    out = jnp.einsum('bnjh,jhd->bnjd', jax.nn.silu(hid), w2)
    return out[:, :, 0], out[:, :, 1]


def nsa_compressed(q, qpos, kc, vc, table, n_sel):
    nbc = kc.shape[1]
    blk_end = (jnp.arange(nbc, dtype=jnp.int32) + 1) * CMP_BLOCK - 1
    dist = qpos[:, None] - blk_end[None, :]
    o, p = mqa_attend(q, kc, vc, dist >= 0, t5_bias(table, dist), NSA_SCALE)
    imp = jnp.pad(p.sum(axis=-3), ((0, 0), (0, 0), (0, n_sel - nbc)))
    blk = jnp.arange(n_sel, dtype=jnp.int32)
    cur = qpos[:, None] // CMP_BLOCK
    forced = (blk == 0) | (blk == cur) | (blk == cur - 1)
    score = jnp.where(blk > cur, NEG, jnp.where(forced, FORCE_SCORE, imp))
    top, idx = lax.top_k(score, min(N_SELECT, n_sel))
    return o, idx, top > NEG / 2


def sel_positions(idx, valid):
    b, nq, k = idx.shape
    kpos = (idx[..., None] * CMP_BLOCK + jnp.arange(CMP_BLOCK, dtype=jnp.int32)).reshape(b, nq, k * CMP_BLOCK)
    return kpos, jnp.repeat(valid, CMP_BLOCK, axis=-1)


def sel_attend(q, qpos, kpos, rvalid, rows, table):
    dist = (qpos[:, None] - kpos)[:, :, None, :]
    mask = rvalid[:, :, None, :] & (dist >= 0)
    o, _ = mqa_attend(q[:, :, None], rows[..., 0, :], rows[..., 1, :], mask, t5_bias(table, dist), NSA_SCALE)
    return o[:, :, 0]


def sel_prompt(q, idx, valid, sel_kv, table):
    b, t = q.shape[:2]
    nqb = t // Q_BLOCK
    qb = q.reshape(b, nqb, Q_BLOCK, NSA_HEADS, NSA_DK).swapaxes(0, 1)
    idxb = idx.reshape(b, nqb, Q_BLOCK, -1).swapaxes(0, 1)
    vb = valid.reshape(b, nqb, Q_BLOCK, -1).swapaxes(0, 1)
    posb = jnp.arange(t, dtype=jnp.int32).reshape(nqb, Q_BLOCK)

    def body(args):
        qi, ii, vi, pi = args
        kpos, rvalid = sel_positions(ii, vi)
        return sel_attend(qi, pi, kpos, rvalid, gather_rows(sel_kv, kpos), table)

    out = lax.map(body, (qb, idxb, vb, posb))
    return out.swapaxes(0, 1).reshape(b, t, NSA_HEADS, NSA_DK)


def sel_sample(q, qpos, idx, valid, cache_nsa, layer, page_table, new_sel, table):
    b = q.shape[0]
    kpos, rvalid = sel_positions(idx, valid)
    pc = jnp.clip(kpos, 0, PAST_LEN - 1)
    phys = jnp.take_along_axis(page_table, (pc // PAGE_SIZE).reshape(b, -1), axis=1).reshape(pc.shape)
    past = cache_nsa[layer, phys, pc % PAGE_SIZE, 2:4]
    new = gather_rows(new_sel, jnp.clip(kpos - PAST_LEN, 0, new_sel.shape[1] - 1))
    rows = jnp.where((kpos < PAST_LEN)[..., None, None], past, new)
    return sel_attend(q, qpos, kpos, rvalid, rows, table)


def win_prompt(q, win_kv, table):
    b, t = q.shape[:2]
    nqb = t // Q_BLOCK
    span = WINDOW + Q_BLOCK
    kvp = jnp.pad(win_kv, ((0, 0), (WINDOW, 0), (0, 0), (0, 0)))
    kidx = jnp.arange(nqb, dtype=jnp.int32)[:, None] * Q_BLOCK + jnp.arange(span, dtype=jnp.int32)
    rows = kvp[:, kidx]
    kpos = kidx - WINDOW
    qpos = jnp.arange(t, dtype=jnp.int32).reshape(nqb, Q_BLOCK)
    dist = qpos[:, :, None] - kpos[:, None, :]
    mask = (kpos[:, None, :] >= 0) & (dist >= 0) & (dist < WINDOW)
    qb = q.reshape(b, nqb, Q_BLOCK, NSA_HEADS, NSA_DK)
    o, _ = mqa_attend(qb, rows[..., 0, :], rows[..., 1, :], mask, t5_bias(table, dist), NSA_SCALE)
    return o.reshape(b, t, NSA_HEADS, NSA_DK)


def win_sample(q, qpos, buf, new_win, table):
    wb = buf.shape[1]
    keys = jnp.concatenate([buf, new_win], axis=1)
    kpos = PAST_LEN - wb + jnp.arange(keys.shape[1], dtype=jnp.int32)
    dist = qpos[:, None] - kpos[None, :]
    mask = (dist >= 0) & (dist < WINDOW)
    o, _ = mqa_attend(q, keys[..., 0, :], keys[..., 1, :], mask, t5_bias(table, dist), NSA_SCALE)
    return o, keys[:, -wb:]


def mla_prompt(mq, mrow):
    b, t = mq.shape[:2]
    nqb = t // Q_BLOCK
    qb = mq.reshape(b, nqb, Q_BLOCK, MLA_HEADS, MLA_DC).swapaxes(0, 1)
    posb = jnp.arange(t, dtype=jnp.int32).reshape(nqb, Q_BLOCK)
    kpos = jnp.arange(t, dtype=jnp.int32)

    def body(args):
        qi, pi = args
        o, _ = mqa_attend(qi, mrow, mrow[..., :KV_LORA], kpos[None, :] <= pi[:, None], None, MLA_SCALE)
        return o

    out = lax.map(body, (qb, posb))
    return out.swapaxes(0, 1).reshape(b, t, MLA_HEADS, KV_LORA)


def mla_sample(mq, qpos, kv):
    kpos = jnp.arange(kv.shape[1], dtype=jnp.int32)
    o, _ = mqa_attend(mq, kv, kv[..., :KV_LORA], kpos[None, :] <= qpos[:, None], None, MLA_SCALE)
    return o


def pre_mix(h, ang, w_in, g_q, g_kv, w_uq, w_ukv):
    b, t, _ = h.shape
    p = split_cols(h @ w_in)
    conv_v = p['conv_c'] * p['conv_u']
    q = p['nsa_q'].reshape(b, t, NSA_HEADS, NSA_DK)
    kv = p['nsa_kv'].reshape(b, t, 6, NSA_DK)
    mq = (rmsnorm(p['mla_cq'], g_q) @ w_uq).reshape(b, t, MLA_HEADS, NOPE_DIM + ROPE_DIM)
    q_lat = jnp.einsum('bthn,chn->bthc', mq[..., :NOPE_DIM], w_ukv[..., :NOPE_DIM])
    mla_q = jnp.concatenate([q_lat, rope(mq[..., NOPE_DIM:], ang[:, None, :])], axis=-1)
    mla_row = jnp.concatenate([rmsnorm(p['mla_ckv'], g_kv), rope(p['mla_kr'], ang)], axis=-1)
    return p, conv_v, q, kv, mla_q, mla_row


def post_mix(p, conv_y, o_c, o_s, o_w, lat, w_ukv, w_branch, w_out):
    b, t = conv_y.shape[:2]
    conv_br = p['conv_b'] * conv_y * jax.nn.silu(p['conv_z'])
    g = jax.nn.sigmoid(p['nsa_g']).reshape(b, t, 3, NSA_HEADS, 1)
    nsa_o = (g[:, :, 0] * o_c + g[:, :, 1] * o_s + g[:, :, 2] * o_w).reshape(b, t, NSA_WIDTH)
    nsa_br = nsa_o * jax.nn.silu(p['nsa_z'])
    mla_o = jnp.einsum('bthc,chv->bthv', lat, w_ukv[..., NOPE_DIM:]).reshape(b, t, MLA_WIDTH)
    mla_br = mla_o * jax.nn.silu(p['mla_z'])
    wc = w_branch[:CONV_DIM]
    wn = w_branch[CONV_DIM:CONV_DIM + NSA_WIDTH]
    wm = w_branch[CONV_DIM + NSA_WIDTH:]
    gm = jax.nn.sigmoid(p['merge_g']).reshape(b, t, N_BRANCH, D_MODEL)
    merged = gm[:, :, 0] * (conv_br @ wc) + gm[:, :, 1] * (nsa_br @ wn) + gm[:, :, 2] * (mla_br @ wm)
    return merged @ w_out


def setup_inputs(seed: int = 0) -> dict:
    key = jax.random.key(seed)
    k = jax.random.split(key, 22)
    f32 = jnp.float32
    n_pages = PAST_LEN // PAGE_SIZE
    n_used = DEC_BATCH * n_pages
    n_pool = n_used + n_used // 4
    win_buf = min(WINDOW, PAST_LEN)

    def nrm(kk, shape, scale):
        return jax.random.normal(kk, shape, f32) * scale

    return {
        'x_prompt': jax.random.normal(k[0], (BATCH, SEQ, D_MODEL), f32),
        'x_sample': jax.random.normal(k[1], (DEC_BATCH, DEC_SEQ, D_MODEL), f32),
        'cache_mla': jax.random.normal(k[2], (DEPTH, n_pool, PAGE_SIZE, MLA_DC), f32),
        'cache_nsa': jax.random.normal(k[3], (DEPTH, n_pool, PAGE_SIZE, 4, NSA_DK), f32),
        'state_nsa_win': jax.random.normal(k[4], (DEPTH, DEC_BATCH, win_buf, 2, NSA_DK), f32),
        'state_conv': jax.random.normal(k[5], (DEPTH, DEC_BATCH, CONV_K - 1, CONV_DIM), f32),
        'page_table': jax.random.permutation(k[6], n_pool)[:n_used].reshape(DEC_BATCH, n_pages).astype(jnp.int32),
        'norm_g': 1.0 + nrm(k[7], (DEPTH, D_MODEL), 0.01),
        'w_in': nrm(k[8], (DEPTH, D_MODEL, IN_COLS), D_MODEL ** -0.5),
        'conv_w': nrm(k[9], (DEPTH, CONV_K, CONV_DIM), CONV_K ** -0.5),
        'phi_pos': nrm(k[10], (DEPTH, 2, CMP_BLOCK, NSA_DK), 0.1),
        'phi_w1': nrm(k[11], (DEPTH, 2, CMP_BLOCK, NSA_DK, PHI_HID), (CMP_BLOCK * NSA_DK) ** -0.5),
        'phi_w2': nrm(k[12], (DEPTH, 2, PHI_HID, NSA_DK), PHI_HID ** -0.5),
        'mla_q_norm': 1.0 + nrm(k[13], (DEPTH, Q_LORA), 0.01),
        'mla_kv_norm': 1.0 + nrm(k[14], (DEPTH, KV_LORA), 0.01),
        'w_uq': nrm(k[15], (DEPTH, Q_LORA, MLA_HEADS * (NOPE_DIM + ROPE_DIM)), Q_LORA ** -0.5),
        'w_ukv': nrm(k[16], (DEPTH, KV_LORA, MLA_HEADS, NOPE_DIM + V_DIM), KV_LORA ** -0.5),
        'rel_bias': nrm(k[17], (N_BUCKETS, NSA_HEADS), 0.5),
        'w_branch': nrm(k[18], (DEPTH, MIX_WIDTH, D_MODEL), (MIX_WIDTH // N_BRANCH) ** -0.5),
        'w_out': nrm(k[19], (DEPTH, D_MODEL, D_MODEL), D_MODEL ** -0.5),
        'final_g': 1.0 + nrm(k[20], (D_MODEL,), 0.01),
    }


def reference(x_prompt, x_sample, cache_mla, cache_nsa, state_nsa_win, state_conv, page_table,
              norm_g, w_in, conv_w, phi_pos, phi_w1, phi_w2, mla_q_norm, mla_kv_norm, w_uq, w_ukv,
              rel_bias, w_branch, w_out, final_g):
    pos_p = jnp.arange(SEQ, dtype=jnp.int32)
    pos_s = PAST_LEN + jnp.arange(DEC_SEQ, dtype=jnp.int32)
    ang_p, ang_s = rope_angles(pos_p), rope_angles(pos_s)
    t_tot = PAST_LEN + DEC_SEQ
    n_new_blk = t_tot // CMP_BLOCK - PAST_LEN // CMP_BLOCK
    n_sel_s = -(-t_tot // CMP_BLOCK)
    xp, xs = x_prompt, x_sample
    mla_p, mla_s, nsa_p, nsa_s, win_p, win_s, conv_p, conv_s = ([] for _ in range(8))
    for l in range(DEPTH):
        lw = (w_in[l], mla_q_norm[l], mla_kv_norm[l], w_uq[l], w_ukv[l])
        phi = (phi_pos[l], phi_w1[l], phi_w2[l])

        p, conv_v, q, kv, mq, mrow = pre_mix(rmsnorm(xp, norm_g[l]), ang_p, *lw)
        conv_y = causal_conv(jnp.pad(conv_v, ((0, 0), (CONV_K - 1, 0), (0, 0))), conv_w[l])
        kc, vc = compress_kv(kv[:, :, 0:2], *phi)
        o_c, idx, valid = nsa_compressed(q, pos_p, kc, vc, rel_bias, SEQ // CMP_BLOCK)
        o_s = sel_prompt(q, idx, valid, kv[:, :, 2:4], rel_bias)
        o_w = win_prompt(q, kv[:, :, 4:6], rel_bias)
        lat = mla_prompt(mq, mrow)
        xp = xp + post_mix(p, conv_y, o_c, o_s, o_w, lat, w_ukv[l], w_branch[l], w_out[l])
        mla_p.append(mrow)
        nsa_p.append(kv[:, :, 0:4])
        win_p.append(kv[:, -min(WINDOW, SEQ):, 4:6])
        conv_p.append(conv_v[:, -(CONV_K - 1):])

        p, conv_v, q, kv, mq, mrow = pre_mix(rmsnorm(xs, norm_g[l]), ang_s, *lw)
        vp = jnp.concatenate([state_conv[l], conv_v], axis=1)
        conv_y = causal_conv(vp, conv_w[l])
        past_cmp = cache_nsa[l, page_table, :, 0:2].reshape(DEC_BATCH, PAST_LEN, 2, NSA_DK)
        kc, vc = compress_kv(past_cmp, *phi)
        if n_new_blk > 0:
            kc_n, vc_n = compress_kv(kv[:, :n_new_blk * CMP_BLOCK, 0:2], *phi)
            kc = jnp.concatenate([kc, kc_n], axis=1)
            vc = jnp.concatenate([vc, vc_n], axis=1)
        o_c, idx, valid = nsa_compressed(q, pos_s, kc, vc, rel_bias, n_sel_s)
        o_s = sel_sample(q, pos_s, idx, valid, cache_nsa, l, page_table, kv[:, :, 2:4], rel_bias)
        o_w, new_win = win_sample(q, pos_s, state_nsa_win[l], kv[:, :, 4:6], rel_bias)
        past_mla = cache_mla[l, page_table].reshape(DEC_BATCH, PAST_LEN, MLA_DC)
        lat = mla_sample(mq, pos_s, jnp.concatenate([past_mla, mrow], axis=1))
        xs = xs + post_mix(p, conv_y, o_c, o_s, o_w, lat, w_ukv[l], w_branch[l], w_out[l])
        mla_s.append(mrow)
        nsa_s.append(kv[:, :, 0:4])
        win_s.append(new_win)
        conv_s.append(vp[:, -(CONV_K - 1):])

    y_prompt = rmsnorm(xp, final_g)
    y_sample = rmsnorm(xs, final_g)
    return (y_prompt, y_sample, jnp.stack(mla_p), jnp.stack(mla_s), jnp.stack(nsa_p), jnp.stack(nsa_s),
            jnp.stack(win_p), jnp.stack(win_s), jnp.stack(conv_p), jnp.stack(conv_s))
```

```python
import functools
import math

import jax
import jax.numpy as jnp
import numpy as np
from jax import lax
from jax.experimental import pallas as pl
from jax.experimental.pallas import tpu as pltpu

F32 = jnp.float32
BF16 = jnp.bfloat16
I32 = jnp.int32

HEADS = 8
DK = 128
ROPE_DIM = 32
ROPE_THETA = 10000.0
CMP_BLOCK = 64
N_SELECT = 16
WINDOW = 512
N_BUCKETS = 32
MAX_DIST = 128
N_BRANCH = 3
RMS_EPS = 1e-6
NEG = -1e30
FORCE_SCORE = 1e4
NSA_SCALE = DK ** -0.5
MLA_SCALE = (DK + ROPE_DIM) ** -0.5
NSA_W = HEADS * DK
KV_COLS = 6 * DK

LANE = 128
TQ = 128
ROWS = TQ * HEADS
MQ_HEAD = 3 * LANE
QPAD = 2 * LANE
VMEM_LIMIT = 56 * 1024 * 1024


def _cp(sem, vmem=VMEM_LIMIT):
    return pltpu.CompilerParams(dimension_semantics=sem, vmem_limit_bytes=vmem)


def _align(x, a):
    return -(-x // a) * a


def _dot(a, b):
    return jnp.dot(a, b, preferred_element_type=F32)


def _dot_t(a, b):
    return lax.dot_general(a, b, (((1,), (1,)), ((), ())), preferred_element_type=F32)


def _silu(x):
    return x / (1.0 + jnp.exp(-x))


def _sigmoid(x):
    return 1.0 / (1.0 + jnp.exp(-x))


def _rms_kernel(x_ref, g_ref, o_ref):
    x = x_ref[...]
    y = x * lax.rsqrt(jnp.mean(x * x, axis=-1, keepdims=True) + RMS_EPS)
    o_ref[...] = (y * g_ref[...]).astype(o_ref.dtype)


def _rmsnorm(x, g, dtype):
    m, d = x.shape
    tm = min(256, m)
    return pl.pallas_call(
        _rms_kernel, grid=(m // tm,),
        in_specs=[pl.BlockSpec((tm, d), lambda i: (i, 0)), pl.BlockSpec((1, d), lambda i: (0, 0))],
        out_specs=pl.BlockSpec((tm, d), lambda i: (i, 0)),
        out_shape=jax.ShapeDtypeStruct((m, d), dtype),
        compiler_params=_cp(("parallel",)), name="rmsnorm")(x, g.reshape(1, d))


def _mm_kernel(a_ref, b_ref, o_ref):
    o_ref[...] = _dot(a_ref[...], b_ref[...]).astype(o_ref.dtype)


def _mm_res_kernel(a_ref, b_ref, r_ref, o_ref):
    o_ref[...] = r_ref[...] + _dot(a_ref[...], b_ref[...])


def _col_tile(n, cap):
    t = min(cap, n) // LANE * LANE
    while n % t:
        t -= LANE
    return t


def _matmul(a, b, res=None, tn_cap=1280, name="matmul"):
    m, k = a.shape
    n = b.shape[1]
    tm = min(512, m)
    tn = _col_tile(n, tn_cap)
    in_specs = [pl.BlockSpec((tm, k), lambda j, i: (i, 0)), pl.BlockSpec((k, tn), lambda j, i: (0, j))]
    args = [a, b]
    kern = _mm_kernel
    if res is not None:
        in_specs.append(pl.BlockSpec((tm, tn), lambda j, i: (i, j)))
        args.append(res)
        kern = _mm_res_kernel
    return pl.pallas_call(
        kern, grid=(n // tn, m // tm), in_specs=in_specs,
        out_specs=pl.BlockSpec((tm, tn), lambda j, i: (i, j)),
        out_shape=jax.ShapeDtypeStruct((m, n), F32),
        compiler_params=_cp(("parallel", "parallel")), name=name)(*args)


def _pack_layout(c, ql, d):
    groups = [("conv_u", c, c), ("conv_b", c, c), ("conv_c", c, c), ("conv_z", c, c),
              ("nsa_q", NSA_W, NSA_W), ("nsa_z", NSA_W, NSA_W), ("mla_z", NSA_W, NSA_W),
              ("merge_g", N_BRANCH * d, min(d, 1024)), ("mla_cq", ql, ql), ("nsa_kv", KV_COLS, KV_COLS),
              ("mla_ckv", DK, LANE), ("g1", LANE, LANE), ("g2", LANE, LANE)]
    off, lay = 0, {}
    for name, w, a in groups:
        off = _align(off, a)
        lay[name] = off
        off += w
    return lay, _align(off, LANE)


def _pack_w_in(w, lay, n_pack, c, ql, d):
    src, off = {}, 0
    for name, n in (("conv_u", c), ("conv_b", c), ("conv_c", c), ("conv_z", c), ("nsa_q", NSA_W),
                    ("nsa_kv", KV_COLS), ("nsa_g", N_BRANCH * HEADS), ("nsa_z", NSA_W), ("mla_cq", ql),
                    ("mla_ckv", DK), ("mla_kr", ROPE_DIM), ("mla_z", NSA_W), ("merge_g", N_BRANCH * d)):
        src[name] = w[:, off:off + n]
        off += n
    half = ROPE_DIM // 2
    kr = src["mla_kr"]
    kr_sw = jnp.concatenate([kr[:, half:], kr[:, :half]], axis=1)
    zeros = lambda n: jnp.zeros((w.shape[0], n), w.dtype)
    dst = dict(src)
    dst["g1"] = jnp.concatenate([kr, src["nsa_g"], zeros(LANE - ROPE_DIM - N_BRANCH * HEADS)], axis=1)
    dst["g2"] = jnp.concatenate([kr_sw, zeros(LANE - ROPE_DIM)], axis=1)
    pieces, pos = [], 0
    for name, o in sorted(lay.items(), key=lambda kv: kv[1]):
        if o > pos:
            pieces.append(zeros(o - pos))
        pieces.append(dst[name])
        pos = o + dst[name].shape[1]
    if n_pack > pos:
        pieces.append(zeros(n_pack - pos))
    return jnp.concatenate(pieces, axis=1).astype(BF16)


def _pack_w_uq(w_uq):
    ql = w_uq.shape[0]
    w = w_uq.reshape(ql, HEADS, DK + ROPE_DIM)
    half = ROPE_DIM // 2
    nope, r = w[..., :DK], w[..., DK:]
    r_sw = jnp.concatenate([r[..., half:], r[..., :half]], axis=-1)
    z = jnp.zeros((ql, HEADS, LANE - ROPE_DIM), w.dtype)
    return jnp.concatenate([nope, r, z, r_sw, z], axis=-1).reshape(ql, HEADS * MQ_HEAD).astype(BF16)


def _rope_tables(pos):
    inv = ROPE_THETA ** (-jnp.arange(0, ROPE_DIM, 2, dtype=F32) / ROPE_DIM)
    ang = jnp.asarray(pos, F32)[:, None] * inv
    c, s = jnp.cos(ang), jnp.sin(ang)
    z = jnp.zeros((ang.shape[0], LANE - ROPE_DIM), F32)
    return jnp.concatenate([c, c, z], axis=1), jnp.concatenate([-s, s, z], axis=1)


def _bucket_of_dist():
    d = np.arange(MAX_DIST + 1)
    exact = N_BUCKETS // 2
    large = exact + np.floor(np.log(np.maximum(d, 1) / exact) / math.log(MAX_DIST / exact)
                             * (N_BUCKETS - exact) + 1e-9).astype(np.int64)
    return np.where(d < exact, d, np.minimum(large, N_BUCKETS - 1)).astype(np.int32)


def _bias_lookup(tbl, dist, valid=None):
    idx = np.clip(dist, 0, MAX_DIST).astype(np.int32)
    b = tbl[:, idx]
    if valid is not None:
        b = jnp.where(jnp.asarray(valid)[None], b, NEG)
    return b


def _prompt_tables(tbl):
    rq = np.arange(TQ)[:, None]
    rk = np.arange(TQ)[None, :]
    d0, d1 = rq - rk, TQ + rq - rk
    hm = lambda b: b.reshape(ROWS, TQ)
    t0 = hm(_bias_lookup(tbl, d0, d0 >= 0))
    t1 = hm(_bias_lookup(tbl, d1))
    far = hm(_bias_lookup(tbl, np.full((TQ, TQ), MAX_DIST)))
    far_edge = hm(_bias_lookup(tbl, np.full((TQ, TQ), MAX_DIST), rk > rq))
    causal = hm(jnp.broadcast_to(jnp.where(jnp.asarray(d0 >= 0), 0.0, NEG), (HEADS, TQ, TQ)))
    sel_tab = jnp.stack([t0, t1, far])
    win_tab = jnp.stack([t0, t1, far, far_edge])
    mla_tab = jnp.stack([causal, jnp.zeros_like(causal)])
    m = np.arange(-2, 2)[None, :]
    dc = np.arange(TQ)[:, None] - CMP_BLOCK * m - (CMP_BLOCK - 1)
    near = _bias_lookup(tbl, dc).reshape(ROWS, 4)
    cmp_near = jnp.concatenate([near, jnp.zeros((ROWS, LANE - 4), F32)], axis=1)
    return sel_tab, win_tab, mla_tab, cmp_near, far


def _sample_tables(tbl, past, nq, nbp):
    nbs = past // CMP_BLOCK
    qi = np.arange(nq)
    qpos = past + qi
    rows = lambda b: jnp.moveaxis(b, 0, 1).reshape(nq * HEADS, -1)
    n = np.arange(nbp)
    dc = qpos[:, None] - (CMP_BLOCK * n[None, :] + CMP_BLOCK - 1)
    cmp_tab = rows(_bias_lookup(tbl, dc, (dc >= 0) & (n[None, :] < nbs)))
    i = np.arange(WINDOW + LANE)
    kpos = past - WINDOW + i
    dw = qpos[:, None] - kpos[None, :]
    win_tab = rows(_bias_lookup(tbl, dw, (dw >= 0) & (dw < WINDOW) & (i[None, :] < WINDOW + nq)))
    r = np.arange(LANE) % CMP_BLOCK
    d_last = (qpos[:, None] - (nbs - 1) * CMP_BLOCK) - r[None, :]
    d_prev = d_last + CMP_BLOCK
    j = np.arange(LANE)
    d_new = qi[:, None] - j[None, :]
    sel_tab = jnp.stack([jnp.moveaxis(_bias_lookup(tbl, d_last), 0, 1),
                         jnp.moveaxis(_bias_lookup(tbl, d_prev), 0, 1),
                         jnp.moveaxis(_bias_lookup(tbl, d_new, (d_new >= 0) & (j[None, :] < nq)), 0, 1),
                         jnp.moveaxis(_bias_lookup(tbl, np.full((nq, LANE), MAX_DIST)), 0, 1)])
    new_mask = jnp.where(jnp.asarray((d_new >= 0) & (j[None, :] < nq)), 0.0, NEG)
    mla_new = jnp.repeat(new_mask, HEADS, axis=0)
    return cmp_tab, win_tab, sel_tab, mla_new


def _pre_kernel(u_ref, c_ref, cq_ref, kv_ref, ckv_ref, g1_ref, g2_ref, gq_ref, gkv_ref, cos_ref, sin_ref,
                convv_ref, cqn_ref, mrow_ref, kpad_ref, kvb_ref):
    convv_ref[...] = c_ref[...] * u_ref[...]
    cq = cq_ref[...]
    cqn = cq * lax.rsqrt(jnp.mean(cq * cq, axis=-1, keepdims=True) + RMS_EPS) * gq_ref[...]
    cqn_ref[...] = cqn.astype(BF16)
    ckv = ckv_ref[...]
    lat = ckv * lax.rsqrt(jnp.mean(ckv * ckv, axis=-1, keepdims=True) + RMS_EPS) * gkv_ref[...]
    roped = g1_ref[...] * cos_ref[...] + g2_ref[...] * sin_ref[...]
    mrow_ref[:, :DK] = lat
    mrow_ref[:, DK:] = roped[:, :ROPE_DIM]
    kpad_ref[...] = jnp.concatenate([lat, roped], axis=1).astype(BF16)
    kvb_ref[...] = kv_ref[...].astype(BF16)


def _pre_mix(p, lay, c, ql, gq, gkv, cos_t, sin_t):
    m = p.shape[0]
    tm = min(256, m)
    col = lambda name, w: pl.BlockSpec((tm, w), functools.partial(lambda i, o: (i, o), o=lay[name] // w))
    row = lambda w: pl.BlockSpec((tm, w), lambda i: (i, 0))
    par = lambda w: pl.BlockSpec((1, w), lambda i: (0, 0))
    return pl.pallas_call(
        _pre_kernel, grid=(m // tm,),
        in_specs=[col("conv_u", c), col("conv_c", c), col("mla_cq", ql), col("nsa_kv", KV_COLS),
                  col("mla_ckv", DK), col("g1", LANE), col("g2", LANE), par(ql), par(DK), row(LANE), row(LANE)],
        out_specs=[row(c), row(ql), row(DK + ROPE_DIM), row(QPAD), row(KV_COLS)],
        out_shape=[jax.ShapeDtypeStruct((m, c), F32), jax.ShapeDtypeStruct((m, ql), BF16),
                   jax.ShapeDtypeStruct((m, DK + ROPE_DIM), F32), jax.ShapeDtypeStruct((m, QPAD), BF16),
                   jax.ShapeDtypeStruct((m, KV_COLS), BF16)],
        compiler_params=_cp(("parallel",)), name="pre_mix",
    )(p, p, p, p, p, p, p, gq.reshape(1, ql), gkv.reshape(1, DK), cos_t, sin_t)


def _mlaq_kernel(mq_ref, wk_ref, cos_ref, sin_ref, q_ref):
    cos_t, sin_t = cos_ref[...], sin_ref[...]
    for h in range(HEADS):
        base = h * MQ_HEAD
        q_lat = _dot(mq_ref[:, base:base + DK].astype(BF16), wk_ref[h])
        roped = mq_ref[:, base + LANE:base + 2 * LANE] * cos_t + mq_ref[:, base + 2 * LANE:base + 3 * LANE] * sin_t
        q_ref[:, h * QPAD:h * QPAD + DK] = q_lat.astype(q_ref.dtype)
        q_ref[:, h * QPAD + DK:(h + 1) * QPAD] = roped.astype(q_ref.dtype)


def _mla_query(mq, wk, cos_t, sin_t, dtype):
    m = mq.shape[0]
    tm = min(256, m)
    return pl.pallas_call(
        _mlaq_kernel, grid=(m // tm,),
        in_specs=[pl.BlockSpec((tm, HEADS * MQ_HEAD), lambda i: (i, 0)),
                  pl.BlockSpec((HEADS, DK, DK), lambda i: (0, 0, 0)),
                  pl.BlockSpec((tm, LANE), lambda i: (i, 0)), pl.BlockSpec((tm, LANE), lambda i: (i, 0))],
        out_specs=pl.BlockSpec((tm, HEADS * QPAD), lambda i: (i, 0)),
        out_shape=jax.ShapeDtypeStruct((m, HEADS * QPAD), dtype),
        compiler_params=_cp(("parallel",)), name="mla_query")(mq, wk, cos_t, sin_t)


def _compress_rows(rows_ref, nb, pos_ref, w1_ref, w2_ref, j):
    hid = w1_ref.shape[-1]
    acc = jnp.zeros((nb, hid), F32)
    for c in range(CMP_BLOCK):
        xc = rows_ref[pl.ds(c, nb, stride=CMP_BLOCK), :] + pos_ref[j, c:c + 1, :]
        acc = acc + _dot(xc.astype(BF16), w1_ref[j, c * DK:(c + 1) * DK, :])
    return _dot(_silu(acc).astype(BF16), w2_ref[j])


def _cmp_prompt_kernel(rows_ref, pos_ref, w1_ref, w2_ref, o_ref):
    nb = o_ref.shape[1]
    o_ref[0] = _compress_rows(rows_ref, nb, pos_ref, w1_ref, w2_ref, 0).astype(o_ref.dtype)


def _compress_prompt(p, lay, pos, w1, w2):
    t = p.shape[0]
    nb = t // CMP_BLOCK
    hid = w1.shape[-1]
    base = lay["nsa_kv"] // DK
    return pl.pallas_call(
        _cmp_prompt_kernel, grid=(2,),
        in_specs=[pl.BlockSpec((t, DK), lambda j: (0, base + j)),
                  pl.BlockSpec((1, CMP_BLOCK, DK), lambda j: (j, 0, 0)),
                  pl.BlockSpec((1, CMP_BLOCK * DK, hid), lambda j: (j, 0, 0)),
                  pl.BlockSpec((1, hid, DK), lambda j: (j, 0, 0))],
        out_specs=pl.BlockSpec((1, nb, DK), lambda j: (j, 0, 0)),
        out_shape=jax.ShapeDtypeStruct((2, nb, DK), BF16),
        compiler_params=_cp(("parallel",)), name="compress_prompt")(p, pos, w1, w2)


def _page_copies(cache_ref, layer, pt_ref, pt_base, n_pages, col0, width, dst, sem):
    page = cache_ref.shape[2]
    copies = []
    for pg in range(n_pages):
        src = cache_ref.at[layer, pt_ref[pt_base + pg]]
        if width != cache_ref.shape[3]:
            src = src.at[:, pl.ds(col0, width)]
        copies.append(pltpu.make_async_copy(src, dst.at[pl.ds(pg * page, page), :], sem))
    return copies


def _cmp_sample_kernel(pt_ref, pos_ref, w1_ref, w2_ref, cache_ref, kc_ref, vc_ref, buf, sem, *, layer, pps):
    s = pl.program_id(0)
    n = pl.num_programs(0)
    slot = s % 2
    fetch = lambda step, sl: [cp for j in range(2) for cp in _page_copies(
        cache_ref, layer, pt_ref, step * pps, pps, j * DK, DK, buf.at[sl, j], sem.at[sl])]

    @pl.when(s == 0)
    def _():
        for cp in fetch(0, 0):
            cp.start()

    @pl.when(s + 1 < n)
    def _():
        for cp in fetch(s + 1, 1 - slot):
            cp.start()

    for cp in fetch(s, slot):
        cp.wait()
    nb = kc_ref.shape[0]
    kc_ref[...] = _compress_rows(buf.at[slot, 0], nb, pos_ref, w1_ref, w2_ref, 0).astype(kc_ref.dtype)
    vc_ref[...] = _compress_rows(buf.at[slot, 1], nb, pos_ref, w1_ref, w2_ref, 1).astype(vc_ref.dtype)


def _compress_sample(pt_flat, cache4, layer, pos, w1, w2, pps):
    page = cache4.shape[2]
    n_tot = pt_flat.shape[0]
    nb = pps * page // CMP_BLOCK
    hid = w1.shape[-1]
    full = lambda shape: pl.BlockSpec(shape, lambda s, pt: (0,) * len(shape))
    grid_spec = pltpu.PrefetchScalarGridSpec(
        num_scalar_prefetch=1, grid=(n_tot // pps,),
        in_specs=[full((2, CMP_BLOCK, DK)), full((2, CMP_BLOCK * DK, hid)), full((2, hid, DK)),
                  pl.BlockSpec(memory_space=pl.ANY)],
        out_specs=[pl.BlockSpec((nb, DK), lambda s, pt: (s, 0)), pl.BlockSpec((nb, DK), lambda s, pt: (s, 0))],
        scratch_shapes=[pltpu.VMEM((2, 2, pps * page, DK), F32), pltpu.SemaphoreType.DMA((2,))])
    n_blocks = n_tot * page // CMP_BLOCK
    return pl.pallas_call(
        functools.partial(_cmp_sample_kernel, layer=layer, pps=pps), grid_spec=grid_spec,
        out_shape=[jax.ShapeDtypeStruct((n_blocks, DK), BF16), jax.ShapeDtypeStruct((n_blocks, DK), BF16)],
        compiler_params=_cp(("arbitrary",)), name="compress_sample")(pt_flat, pos, w1, w2, cache4)


def _select_blocks(score, blk):
    sel = jnp.zeros(score.shape, F32)
    picks = []
    for _ in range(N_SELECT):
        top = jnp.max(score, axis=-1, keepdims=True)
        pick = jnp.min(jnp.where(score == top, blk, jnp.int32(1 << 30)), axis=-1, keepdims=True)
        hit = blk == pick
        sel = jnp.where(hit & (top > NEG / 2), 1.0, sel)
        score = jnp.where(hit, -3e38, score)
        picks.append(pick)
    return sel, picks


def _block_scores(imp, blk, cur):
    forced = (blk == 0) | (blk == cur) | (blk == cur - 1)
    return jnp.where(blk > cur, NEG, jnp.where(forced, FORCE_SCORE, imp))


def _cmp_attn_prompt_kernel(q_ref, kc_ref, vc_ref, near_ref, far_ref, o_ref, sel_ref):
    qt = pl.program_id(0)
    nbp = kc_ref.shape[0]
    q = jnp.concatenate([q_ref[:, h * DK:(h + 1) * DK].astype(BF16) for h in range(HEADS)], axis=0)
    s = _dot_t(q, kc_ref[...]) * NSA_SCALE
    lane = lax.broadcasted_iota(I32, (ROWS, nbp), 1)
    qpos = qt * TQ + (lax.broadcasted_iota(I32, (ROWS, nbp), 0) & (TQ - 1))
    bias = jnp.broadcast_to(far_ref[:, 0:1], (ROWS, nbp))
    for i in range(4):
        bias = jnp.where(lane == 2 * qt + (i - 2), near_ref[:, i:i + 1], bias)
    valid = CMP_BLOCK * lane + (CMP_BLOCK - 1) <= qpos
    s = jnp.where(valid, s + bias, NEG)
    e = jnp.where(valid, jnp.exp(s - jnp.max(s, axis=-1, keepdims=True)), 0.0)
    den = jnp.sum(e, axis=-1, keepdims=True)
    p = e / jnp.where(den > 0.0, den, 1.0)
    o = _dot(p.astype(BF16), vc_ref[...])
    imp = p[0:TQ]
    for h in range(1, HEADS):
        imp = imp + p[h * TQ:(h + 1) * TQ]
    for h in range(HEADS):
        o_ref[:, h * DK:(h + 1) * DK] = o[h * TQ:(h + 1) * TQ]
    blk = lax.broadcasted_iota(I32, (TQ, nbp), 1)
    cur = (qt * TQ + lax.broadcasted_iota(I32, (TQ, nbp), 0)) // CMP_BLOCK
    sel, _ = _select_blocks(_block_scores(imp, blk, cur), blk)
    sel_ref[...] = sel


def _cmp_attn_prompt(p, lay, kc, vc, near, far):
    t = p.shape[0]
    nbp = kc.shape[0]
    full = lambda a: pl.BlockSpec(a.shape, lambda i: (0,) * a.ndim)
    return pl.pallas_call(
        _cmp_attn_prompt_kernel, grid=(t // TQ,),
        in_specs=[pl.BlockSpec((TQ, NSA_W), functools.partial(lambda i, o: (i, o), o=lay["nsa_q"] // NSA_W)),
                  full(kc), full(vc), full(near), full(far)],
        out_specs=[pl.BlockSpec((TQ, NSA_W), lambda i: (i, 0)), pl.BlockSpec((TQ, nbp), lambda i: (i, 0))],
        out_shape=[jax.ShapeDtypeStruct((t, NSA_W), F32), jax.ShapeDtypeStruct((t, nbp), F32)],
        compiler_params=_cp(("parallel",)), name="cmp_attn_prompt")(p, kc, vc, near, far)


def _attn_prompt_kernel(*refs, dq, scale, mode):
    if mode == "sel":
        q_ref, k_ref, v_ref, tab_ref, sel_ref, o_ref, q_scr, m_scr, l_scr, acc_scr = refs
    else:
        q_ref, k_ref, v_ref, tab_ref, o_ref, q_scr, m_scr, l_scr, acc_scr = refs
    qt = pl.program_id(0)
    for h in range(HEADS):
        q_scr[h * TQ:(h + 1) * TQ, :] = q_ref[:, h * dq:(h + 1) * dq].astype(BF16)
    m_scr[...] = jnp.full(m_scr.shape, NEG, F32)
    l_scr[...] = jnp.zeros(l_scr.shape, F32)
    acc_scr[...] = jnp.zeros(acc_scr.shape, F32)
    n_win = WINDOW // TQ
    lo = jnp.maximum(qt - n_win, 0) if mode == "win" else 0

    def body(kt, carry):
        row = pl.multiple_of(kt * TQ, TQ)
        k = k_ref[pl.ds(row, TQ), :]
        v = v_ref[pl.ds(row, TQ), :]
        delta = qt - kt
        if mode == "mla":
            tab = jnp.minimum(delta, 1)
        elif mode == "sel":
            tab = jnp.minimum(delta, 2)
        else:
            tab = jnp.where(delta == n_win, 3, jnp.minimum(delta, 2))
        s = _dot_t(q_scr[...], k) * scale + tab_ref[tab]
        if mode == "sel":
            nbp = sel_ref.shape[1]
            blk = lax.broadcasted_iota(I32, (nbp, TQ), 0)
            key = lax.broadcasted_iota(I32, (nbp, TQ), 1)
            expand = jnp.where(blk == 2 * kt + key // CMP_BLOCK, 1.0, 0.0).astype(BF16)
            picked = _dot(sel_ref[...].astype(BF16), expand)
            s = jnp.where(jnp.concatenate([picked] * HEADS, axis=0) > 0.5, s, NEG)
        m_prev = m_scr[...]
        m_new = jnp.maximum(m_prev, jnp.max(s, axis=-1, keepdims=True))
        alpha = jnp.exp(m_prev - m_new)
        e = jnp.exp(s - m_new)
        l_scr[...] = alpha * l_scr[...] + jnp.sum(e, axis=-1, keepdims=True)
        acc_scr[...] = alpha * acc_scr[...] + _dot(e.astype(BF16), v)
        m_scr[...] = m_new
        return carry

    lax.fori_loop(lo, qt + 1, body, 0)
    o = acc_scr[...] / l_scr[...]
    for h in range(HEADS):
        o_ref[:, h * DK:(h + 1) * DK] = o[h * TQ:(h + 1) * TQ]


def _attn_prompt(q, q_col, dq, k, k_col, v, v_col, tab, scale, mode, sel=None):
    t = k.shape[0]
    dk = dq
    full = lambda a: pl.BlockSpec(a.shape, lambda i: (0,) * a.ndim)
    in_specs = [pl.BlockSpec((TQ, HEADS * dq), functools.partial(lambda i, o: (i, o), o=q_col)),
                pl.BlockSpec((t, dk), functools.partial(lambda i, o: (0, o), o=k_col)),
                pl.BlockSpec((t, DK), functools.partial(lambda i, o: (0, o), o=v_col)),
                full(tab)]
    args = [q, k, v, tab]
    if mode == "sel":
        in_specs.append(pl.BlockSpec((TQ, sel.shape[1]), lambda i: (i, 0)))
        args.append(sel)
    return pl.pallas_call(
        functools.partial(_attn_prompt_kernel, dq=dq, scale=scale, mode=mode), grid=(t // TQ,),
        in_specs=in_specs,
        out_specs=pl.BlockSpec((TQ, NSA_W), lambda i: (i, 0)),
        out_shape=jax.ShapeDtypeStruct((t, NSA_W), F32),
        scratch_shapes=[pltpu.VMEM((ROWS, dq), BF16), pltpu.VMEM((ROWS, 1), F32), pltpu.VMEM((ROWS, 1), F32),
                        pltpu.VMEM((ROWS, DK), F32)],
        compiler_params=_cp(("parallel",)), name="attn_prompt_" + mode)(*args)


def _softmax_rows(s):
    e = jnp.exp(s - jnp.max(s, axis=-1, keepdims=True))
    return e / jnp.sum(e, axis=-1, keepdims=True)


def _cmp_attn_sample_kernel(q_ref, kc_ref, vc_ref, tab_ref, o_ref, idx_ref, *, nbs):
    nq = q_ref.shape[0]
    nbp = kc_ref.shape[0]
    q = q_ref[...].reshape(nq * HEADS, DK).astype(BF16)
    p = _softmax_rows(_dot_t(q, kc_ref[...]) * NSA_SCALE + tab_ref[...])
    o_ref[...] = _dot(p.astype(BF16), vc_ref[...]).reshape(nq, HEADS, DK)
    imp = jnp.sum(p.reshape(nq, HEADS, nbp), axis=1)
    width = _align(nbs + 1, LANE)
    if width > nbp:
        imp = jnp.concatenate([imp, jnp.zeros((nq, width - nbp), F32)], axis=1)
    blk = lax.broadcasted_iota(I32, (nq, width), 1)
    imp = jnp.where(blk < nbs, imp, 0.0)
    _, picks = _select_blocks(_block_scores(imp, blk, nbs), blk)
    lane = lax.broadcasted_iota(I32, (nq, LANE), 1)
    out = jnp.zeros((nq, LANE), I32)
    for i, pick in enumerate(picks):
        out = jnp.where(lane == i, pick, out)
    idx_ref[0] = out


def _cmp_attn_sample(q3, kc, vc, tab, nq, nbs):
    ms = q3.shape[0]
    b = ms // nq
    nbp = kc.shape[0] // b
    return pl.pallas_call(
        functools.partial(_cmp_attn_sample_kernel, nbs=nbs), grid=(b,),
        in_specs=[pl.BlockSpec((nq, HEADS, DK), lambda i: (i, 0, 0)),
                  pl.BlockSpec((nbp, DK), lambda i: (i, 0)), pl.BlockSpec((nbp, DK), lambda i: (i, 0)),
                  pl.BlockSpec(tab.shape, lambda i: (0, 0))],
        out_specs=[pl.BlockSpec((nq, HEADS, DK), lambda i: (i, 0, 0)), pl.BlockSpec((1, nq, LANE), lambda i: (i, 0, 0))],
        out_shape=[jax.ShapeDtypeStruct((ms, HEADS, DK), F32), jax.ShapeDtypeStruct((b, nq, LANE), I32)],
        compiler_params=_cp(("parallel",)), name="cmp_attn_sample")(q3, kc, vc, tab)


def _pad_new_rows(pad_scr, new):
    pad_scr[...] = jnp.zeros(pad_scr.shape, F32)
    pad_scr[0:new.shape[0], :] = new
    return pad_scr[...]


def _win_sample_kernel(q_ref, st_ref, new_ref, tab_ref, o_ref, pad_scr):
    nq = q_ref.shape[0]
    q = q_ref[...].reshape(nq * HEADS, DK).astype(BF16)
    st = st_ref[0]
    newp = _pad_new_rows(pad_scr, new_ref[0])
    s = jnp.concatenate([_dot_t(q, st[:, :DK].astype(BF16)), _dot_t(q, newp[:, :DK].astype(BF16))], axis=1)
    p = _softmax_rows(s * NSA_SCALE + tab_ref[...]).astype(BF16)
    o = _dot(p[:, :WINDOW], st[:, DK:].astype(BF16)) + _dot(p[:, WINDOW:], newp[:, DK:].astype(BF16))
    o_ref[...] = o.reshape(nq, HEADS, DK)


def _win_sample(q3, state, new, tab, nq):
    ms = q3.shape[0]
    return pl.pallas_call(
        _win_sample_kernel, grid=(ms // nq,),
        in_specs=[pl.BlockSpec((nq, HEADS, DK), lambda i: (i, 0, 0)),
                  pl.BlockSpec((1, WINDOW, 2 * DK), lambda i: (i, 0, 0)),
                  pl.BlockSpec((1, nq, 2 * DK), lambda i: (i, 0, 0)),
                  pl.BlockSpec(tab.shape, lambda i: (0, 0))],
        out_specs=pl.BlockSpec((nq, HEADS, DK), lambda i: (i, 0, 0)),
        out_shape=jax.ShapeDtypeStruct((ms, HEADS, DK), F32),
        scratch_shapes=[pltpu.VMEM((LANE, 2 * DK), F32)],
        compiler_params=_cp(("parallel",)), name="win_sample")(q3, state, new, tab)


def _sel_copies(cache_ref, layer, idx_ref, pt_ref, step, nq, n_pages, nbs, buf, slot, sem):
    per_page = cache_ref.shape[2] // CMP_BLOCK
    copies = []
    for i in range(nq * N_SELECT):
        j = jnp.minimum(idx_ref[step * nq * N_SELECT + i], nbs - 1)
        page = pt_ref[step * n_pages + j // per_page]
        r0 = pl.multiple_of((j % per_page) * CMP_BLOCK, CMP_BLOCK)
        src = cache_ref.at[layer, page, pl.ds(r0, CMP_BLOCK), pl.ds(2 * DK, 2 * DK)]
        copies.append(pltpu.make_async_copy(src, buf.at[slot, pl.ds(i * CMP_BLOCK, CMP_BLOCK), :], sem.at[slot]))
    return copies


def _sel_sample_kernel(idx_ref, pt_ref, q_ref, new_ref, tab_ref, cache_ref, o_ref, buf, sem, pad_scr,
                       *, layer, n_pages, nbs):
    s = pl.program_id(0)
    n = pl.num_programs(0)
    nq = q_ref.shape[0]
    slot = s % 2
    fetch = lambda step, sl: _sel_copies(cache_ref, layer, idx_ref, pt_ref, step, nq, n_pages, nbs, buf, sl, sem)

    @pl.when(s == 0)
    def _():
        for cp in fetch(0, 0):
            cp.start()

    @pl.when(s + 1 < n)
    def _():
        for cp in fetch(s + 1, 1 - slot):
            cp.start()

    for cp in fetch(s, slot):
        cp.wait()
    newp = _pad_new_rows(pad_scr, new_ref[0])
    kn, vn = newp[:, :DK].astype(BF16), newp[:, DK:].astype(BF16)
    lane = lax.broadcasted_iota(I32, (HEADS, LANE), 1)
    span = N_SELECT * CMP_BLOCK
    for qi in range(nq):
        q = q_ref[qi].astype(BF16)
        kv = buf[slot, qi * span:(qi + 1) * span, :]
        k, v = kv[:, :DK].astype(BF16), kv[:, DK:].astype(BF16)

        def block_bias(i):
            j = idx_ref[(s * nq + qi) * N_SELECT + i]
            near = jnp.where(j == nbs - 1, tab_ref[0, qi], jnp.where(j == nbs - 2, tab_ref[1, qi], tab_ref[3, qi]))
            return jnp.where(j >= nbs, NEG, near)

        bias = jnp.concatenate([jnp.where(lane < CMP_BLOCK, block_bias(2 * g), block_bias(2 * g + 1))
                                for g in range(N_SELECT // 2)], axis=1)
        sc = jnp.concatenate([_dot_t(q, k) * NSA_SCALE + bias, _dot_t(q, kn) * NSA_SCALE + tab_ref[2, qi]], axis=1)
        p = _softmax_rows(sc).astype(BF16)
        o_ref[qi] = _dot(p[:, :span], v) + _dot(p[:, span:], vn)


def _sel_sample(idx_flat, pt_flat, q3, new, tab, cache4, layer, nq, n_pages, nbs):
    ms = q3.shape[0]
    grid_spec = pltpu.PrefetchScalarGridSpec(
        num_scalar_prefetch=2, grid=(ms // nq,),
        in_specs=[pl.BlockSpec((nq, HEADS, DK), lambda i, a, b: (i, 0, 0)),
                  pl.BlockSpec((1, nq, 2 * DK), lambda i, a, b: (i, 0, 0)),
                  pl.BlockSpec(tab.shape, lambda i, a, b: (0, 0, 0, 0)),
                  pl.BlockSpec(memory_space=pl.ANY)],
        out_specs=pl.BlockSpec((nq, HEADS, DK), lambda i, a, b: (i, 0, 0)),
        scratch_shapes=[pltpu.VMEM((2, nq * N_SELECT * CMP_BLOCK, 2 * DK), F32), pltpu.SemaphoreType.DMA((2,)),
                        pltpu.VMEM((LANE, 2 * DK), F32)])
    return pl.pallas_call(
        functools.partial(_sel_sample_kernel, layer=layer, n_pages=n_pages, nbs=nbs), grid_spec=grid_spec,
        out_shape=jax.ShapeDtypeStruct((ms, HEADS, DK), F32),
        compiler_params=_cp(("arbitrary",)), name="sel_sample")(idx_flat, pt_flat, q3, new, tab, cache4)


def _mla_sample_kernel(pt_ref, q_ref, new_ref, tab_ref, cache_ref, o_ref, buf, sem, pad_scr, m_scr, l_scr, acc_scr,
                       *, layer, pps, n_chunk, chunk_rows):
    s = pl.program_id(0)
    n = pl.num_programs(0)
    nq = q_ref.shape[0]
    slot = s % 2
    ch = s % n_chunk
    width = cache_ref.shape[3]
    fetch = lambda step, sl: _page_copies(cache_ref, layer, pt_ref, step * pps, pps, 0, width, buf.at[sl], sem.at[sl])

    @pl.when(s == 0)
    def _():
        for cp in fetch(0, 0):
            cp.start()

    @pl.when(s + 1 < n)
    def _():
        for cp in fetch(s + 1, 1 - slot):
            cp.start()

    for cp in fetch(s, slot):
        cp.wait()

    @pl.when(ch == 0)
    def _():
        m_scr[...] = jnp.full(m_scr.shape, NEG, F32)
        l_scr[...] = jnp.zeros(l_scr.shape, F32)
        acc_scr[...] = jnp.zeros(acc_scr.shape, F32)

    q = q_ref[...].reshape(nq * HEADS, QPAD).astype(BF16)
    zpad = jnp.zeros((chunk_rows, QPAD - width), F32)

    def update(kp, bias):
        sc = _dot_t(q, kp) * MLA_SCALE
        if bias is not None:
            sc = sc + bias
        m_prev = m_scr[...]
        m_new = jnp.maximum(m_prev, jnp.max(sc, axis=-1, keepdims=True))
        alpha = jnp.exp(m_prev - m_new)
        e = jnp.exp(sc - m_new)
        l_scr[...] = alpha * l_scr[...] + jnp.sum(e, axis=-1, keepdims=True)
        acc_scr[...] = alpha * acc_scr[...] + _dot(e.astype(BF16), kp[:, :DK])
        m_scr[...] = m_new

    def body(c, carry):
        row = pl.multiple_of(c * chunk_rows, chunk_rows)
        rows = buf[slot, pl.ds(row, chunk_rows), :]
        update(jnp.concatenate([rows, zpad], axis=1).astype(BF16), None)
        return carry

    lax.fori_loop(0, buf.shape[1] // chunk_rows, body, 0)

    @pl.when(ch == n_chunk - 1)
    def _():
        update(_pad_new_rows(pad_scr, new_ref[0]).astype(BF16), tab_ref[...])
        o_ref[...] = (acc_scr[...] / l_scr[...]).reshape(nq, HEADS, DK)


def _mla_sample(pt_flat, q3, new, tab, cache, layer, nq, n_pages, pps):
    ms = q3.shape[0]
    b = ms // nq
    page, width = cache.shape[2], cache.shape[3]
    n_chunk = n_pages // pps
    chunk_rows = min(4, pps) * page
    grid_spec = pltpu.PrefetchScalarGridSpec(
        num_scalar_prefetch=1, grid=(b * n_chunk,),
        in_specs=[pl.BlockSpec((nq, HEADS, QPAD), lambda s, pt: (s // n_chunk, 0, 0)),
                  pl.BlockSpec((1, nq, QPAD), lambda s, pt: (s // n_chunk, 0, 0)),
                  pl.BlockSpec(tab.shape, lambda s, pt: (0, 0)),
                  pl.BlockSpec(memory_space=pl.ANY)],
        out_specs=pl.BlockSpec((nq, HEADS, DK), lambda s, pt: (s // n_chunk, 0, 0)),
        scratch_shapes=[pltpu.VMEM((2, pps * page, width), F32), pltpu.SemaphoreType.DMA((2,)),
                        pltpu.VMEM((LANE, QPAD), F32), pltpu.VMEM((nq * HEADS, 1), F32),
                        pltpu.VMEM((nq * HEADS, 1), F32), pltpu.VMEM((nq * HEADS, DK), F32)])
    return pl.pallas_call(
        functools.partial(_mla_sample_kernel, layer=layer, pps=pps, n_chunk=n_chunk, chunk_rows=chunk_rows),
        grid_spec=grid_spec, out_shape=jax.ShapeDtypeStruct((ms, HEADS, DK), F32),
        compiler_params=_cp(("arbitrary",)), name="mla_sample")(pt_flat, q3, new, tab, cache)


def _post_kernel(oc_ref, os_ref, ow_ref, g1_ref, nz_ref, mz_ref, lat_ref, wv_ref, v0_ref, v1_ref, v2_ref,
                 cb_ref, cz_ref, cw_ref, brc_ref, brn_ref, brm_ref):
    g = _sigmoid(g1_ref[...])
    for h in range(HEADS):
        hs = slice(h * DK, (h + 1) * DK)
        gate = lambda br: g[:, ROPE_DIM + br * HEADS + h:ROPE_DIM + br * HEADS + h + 1]
        o = gate(0) * oc_ref[:, hs] + gate(1) * os_ref[:, hs] + gate(2) * ow_ref[:, hs]
        brn_ref[:, hs] = (o * _silu(nz_ref[:, hs])).astype(BF16)
        mo = _dot(lat_ref[:, hs].astype(BF16), wv_ref[h])
        brm_ref[:, hs] = (mo * _silu(mz_ref[:, hs])).astype(BF16)
    y = cw_ref[0:1, :] * v0_ref[...] + cw_ref[1:2, :] * v1_ref[...] + cw_ref[2:3, :] * v2_ref[...]
    brc_ref[...] = (cb_ref[...] * y * _silu(cz_ref[...])).astype(BF16)


def _post_mix(o_c, o_s, o_w, p, lay, lat, wv, v0, v1, v2, conv_w, c):
    m = p.shape[0]
    tm = min(256, m)
    col = lambda name, w: pl.BlockSpec((tm, w), functools.partial(lambda i, o: (i, o), o=lay[name] // w))
    row = lambda w: pl.BlockSpec((tm, w), lambda i: (i, 0))
    return pl.pallas_call(
        _post_kernel, grid=(m // tm,),
        in_specs=[row(NSA_W), row(NSA_W), row(NSA_W), col("g1", LANE), col("nsa_z", NSA_W), col("mla_z", NSA_W),
                  row(NSA_W), pl.BlockSpec((HEADS, DK, DK), lambda i: (0, 0, 0)), row(c), row(c), row(c),
                  col("conv_b", c), col("conv_z", c), pl.BlockSpec(conv_w.shape, lambda i: (0, 0))],
        out_specs=[row(c), row(NSA_W), row(NSA_W)],
        out_shape=[jax.ShapeDtypeStruct((m, c), BF16), jax.ShapeDtypeStruct((m, NSA_W), BF16),
                   jax.ShapeDtypeStruct((m, NSA_W), BF16)],
        compiler_params=_cp(("parallel",)), name="post_mix",
    )(o_c, o_s, o_w, p, p, p, lat, wv, v0, v1, v2, p, p, conv_w)


def _merge_kernel(brc_ref, brn_ref, brm_ref, wc_ref, wn_ref, wm_ref, g0_ref, g1_ref, g2_ref, o_ref):
    merged = (_sigmoid(g0_ref[...]) * _dot(brc_ref[...], wc_ref[...])
              + _sigmoid(g1_ref[...]) * _dot(brn_ref[...], wn_ref[...])
              + _sigmoid(g2_ref[...]) * _dot(brm_ref[...], wm_ref[...]))
    o_ref[...] = merged.astype(BF16)


def _merge(brc, brn, brm, wc, wn, wm, p, lay, d):
    m = p.shape[0]
    tm = min(512, m)
    tn = _col_tile(d, 512)
    act = lambda a: pl.BlockSpec((tm, a.shape[1]), lambda j, i: (i, 0))
    wgt = lambda a: pl.BlockSpec((a.shape[0], tn), lambda j, i: (0, j))
    gate = lambda br: pl.BlockSpec((tm, tn), functools.partial(lambda j, i, o: (i, o + j), o=(lay["merge_g"] + br * d) // tn))
    return pl.pallas_call(
        _merge_kernel, grid=(d // tn, m // tm),
        in_specs=[act(brc), act(brn), act(brm), wgt(wc), wgt(wn), wgt(wm), gate(0), gate(1), gate(2)],
        out_specs=pl.BlockSpec((tm, tn), lambda j, i: (i, j)),
        out_shape=jax.ShapeDtypeStruct((m, d), BF16),
        compiler_params=_cp(("parallel", "parallel")), name="merge")(brc, brn, brm, wc, wn, wm, p, p, p)


def _project(x, lw):
    h = _rmsnorm(x, lw["norm_g"], BF16)
    p = _matmul(h, lw["w_in"], name="in_proj")
    conv_v, cqn, mrow, kpad, kvb = _pre_mix(p, lw["lay"], lw["c"], lw["ql"], lw["g_q"], lw["g_kv"], lw["cos"], lw["sin"])
    mq = _matmul(cqn, lw["w_uq"], tn_cap=1024, name="q_up_proj")
    return p, conv_v, mrow, kpad, kvb, mq


def _finish(x, p, lw, o_c, o_s, o_w, lat, v0, v1, v2):
    brc, brn, brm = _post_mix(o_c, o_s, o_w, p, lw["lay"], lat, lw["wv"], v0, v1, v2, lw["conv_w"], lw["c"])
    merged = _merge(brc, brn, brm, lw["wc"], lw["wn"], lw["wm"], p, lw["lay"], x.shape[1])
    return _matmul(merged, lw["w_out"], res=x, tn_cap=1024, name="out_proj")


def _prompt_layer(x, lw, tabs):
    lay, c = lw["lay"], lw["c"]
    t = x.shape[0]
    p, conv_v, mrow, kpad, kvb, mq = _project(x, lw)
    q_mla = _mla_query(mq, lw["wk"], lw["cos"], lw["sin"], BF16)
    zc = lambda n: jnp.zeros((n, c), F32)
    v1 = jnp.concatenate([zc(1), conv_v[:-1]], axis=0)
    v0 = jnp.concatenate([zc(2), conv_v[:-2]], axis=0)
    sel_tab, win_tab, mla_tab, cmp_near, far = tabs
    kcv = _compress_prompt(p, lay, lw["phi_pos"], lw["phi_w1"], lw["phi_w2"])
    nbp = _align(kcv.shape[1], LANE)
    kcv = jnp.pad(kcv, ((0, 0), (0, nbp - kcv.shape[1]), (0, 0)))
    o_c, sel = _cmp_attn_prompt(p, lay, kcv[0], kcv[1], cmp_near, far)
    q_col = lay["nsa_q"] // NSA_W
    o_s = _attn_prompt(p, q_col, DK, kvb, 2, kvb, 3, sel_tab, NSA_SCALE, "sel", sel)
    o_w = _attn_prompt(p, q_col, DK, kvb, 4, kvb, 5, win_tab, NSA_SCALE, "win")
    lat = _attn_prompt(q_mla, 0, QPAD, kpad, 0, kpad, 0, mla_tab, MLA_SCALE, "mla")
    x_new = _finish(x, p, lw, o_c, o_s, o_w, lat, v0, v1, conv_v)
    kv0 = lay["nsa_kv"]
    wb = min(WINDOW, t)
    new_nsa = p[:, kv0:kv0 + 4 * DK].reshape(1, t, 4, DK)
    new_win = p[t - wb:, kv0 + 4 * DK:kv0 + 6 * DK].reshape(1, wb, 2, DK)
    return x_new, mrow[None], new_nsa, new_win, conv_v[None, -2:]


def _sample_layer(x, lw, tabs, layer, cache_mla, cache4, state_win, state_conv, pt_flat, b, nq, n_pages):
    lay, c = lw["lay"], lw["c"]
    ms = x.shape[0]
    page = cache4.shape[2]
    past = n_pages * page
    nbs = past // CMP_BLOCK
    cmp_tab, win_tab, sel_tab, mla_new = tabs
    p, conv_v, mrow, kpad, kvb, mq = _project(x, lw)
    q_mla = _mla_query(mq, lw["wk"], lw["cos"], lw["sin"], F32).reshape(ms, HEADS, QPAD)
    vp = jnp.concatenate([state_conv, conv_v.reshape(b, nq, c)], axis=1)
    v0, v1, v2 = (vp[:, i:i + nq].reshape(ms, c) for i in range(3))
    q3 = p[:, lay["nsa_q"]:lay["nsa_q"] + NSA_W].reshape(ms, HEADS, DK)
    kv0 = lay["nsa_kv"]
    kv_new = p[:, kv0:kv0 + KV_COLS].reshape(b, nq, 6, DK)
    kc, vc = _compress_sample(pt_flat, cache4, layer, lw["phi_pos"], lw["phi_w1"], lw["phi_w2"], min(64, n_pages))
    nbp = cmp_tab.shape[1]
    padb = lambda a: jnp.pad(a.reshape(b, nbs, DK), ((0, 0), (0, nbp - nbs), (0, 0))).reshape(b * nbp, DK)
    o_c, idx = _cmp_attn_sample(q3, padb(kc), padb(vc), cmp_tab, nq, nbs)
    idx_flat = idx[:, :, :N_SELECT].reshape(-1)
    o_s = _sel_sample(idx_flat, pt_flat, q3, kv_new[:, :, 2:4].reshape(b, nq, 2 * DK), sel_tab, cache4, layer,
                      nq, n_pages, nbs)
    win_new = kv_new[:, :, 4:6]
    o_w = _win_sample(q3, state_win.reshape(b, WINDOW, 2 * DK), win_new.reshape(b, nq, 2 * DK), win_tab, nq)
    mrow_pad = jnp.pad(mrow, ((0, 0), (0, QPAD - mrow.shape[1]))).reshape(b, nq, QPAD)
    lat = _mla_sample(pt_flat, q_mla, mrow_pad, mla_new, cache_mla, layer, nq, n_pages, min(32, n_pages))
    flat = lambda a: a.reshape(ms, NSA_W)
    x_new = _finish(x, p, lw, flat(o_c), flat(o_s), flat(o_w), flat(lat), v0, v1, v2)
    new_win_state = jnp.concatenate([state_win, win_new], axis=1)[:, -WINDOW:]
    return (x_new, mrow.reshape(b, nq, -1), kv_new[:, :, 0:4], new_win_state, vp[:, -2:])


def kernel(x_prompt, x_sample, cache_mla, cache_nsa, state_nsa_win, state_conv, page_table, norm_g, w_in, conv_w,
           phi_pos, phi_w1, phi_w2, mla_q_norm, mla_kv_norm, w_uq, w_ukv, rel_bias, w_branch, w_out, final_g):
    depth = w_in.shape[0]
    bp, t, d = x_prompt.shape
    b, nq, _ = x_sample.shape
    c = conv_w.shape[-1]
    ql = mla_q_norm.shape[-1]
    hid = phi_w2.shape[2]
    n_pages = page_table.shape[1]
    page = cache_mla.shape[2]
    past = n_pages * page
    assert bp == 1 and t % TQ == 0 and t >= WINDOW and WINDOW % TQ == 0
    assert state_nsa_win.shape[2] == WINDOW and past >= WINDOW and past % CMP_BLOCK == 0
    assert (past + nq) // CMP_BLOCK == past // CMP_BLOCK and past // CMP_BLOCK >= N_SELECT
    assert page % CMP_BLOCK == 0 and nq <= 8 and w_ukv.shape[1:] == (DK, HEADS, 2 * DK)

    lay, n_pack = _pack_layout(c, ql, d)
    tbl = rel_bias[jnp.asarray(_bucket_of_dist())].T.astype(F32)
    nbs = past // CMP_BLOCK
    tabs_p = _prompt_tables(tbl)
    tabs_s = _sample_tables(tbl, past, nq, _align(nbs, LANE))
    cos_p, sin_p = _rope_tables(np.arange(t))
    cos_s, sin_s = _rope_tables(np.tile(past + np.arange(nq), b))
    cache4 = cache_nsa.reshape(cache_nsa.shape[0], cache_nsa.shape[1], page, 4 * DK)
    pt_flat = page_table.reshape(-1)

    xp = x_prompt.reshape(t, d)
    xs = x_sample.reshape(b * nq, d)
    outs = [[] for _ in range(8)]
    for l in range(depth):
        lw = {
            "lay": lay, "c": c, "ql": ql, "norm_g": norm_g[l],
            "w_in": _pack_w_in(w_in[l], lay, n_pack, c, ql, d),
            "g_q": mla_q_norm[l], "g_kv": mla_kv_norm[l], "w_uq": _pack_w_uq(w_uq[l]),
            "wk": jnp.transpose(w_ukv[l][:, :, :DK], (1, 2, 0)).astype(BF16),
            "wv": jnp.transpose(w_ukv[l][:, :, DK:], (1, 0, 2)).astype(BF16),
            "phi_pos": phi_pos[l], "phi_w1": phi_w1[l].reshape(2, CMP_BLOCK * DK, hid).astype(BF16),
            "phi_w2": phi_w2[l].astype(BF16), "conv_w": conv_w[l],
            "wc": w_branch[l][:c].astype(BF16), "wn": w_branch[l][c:c + NSA_W].astype(BF16),
            "wm": w_branch[l][c + NSA_W:].astype(BF16), "w_out": w_out[l].astype(BF16),
        }
        xp, mla_p, nsa_p, win_p, conv_p = _prompt_layer(xp, dict(lw, cos=cos_p, sin=sin_p), tabs_p)
        xs, mla_s, nsa_s, win_s, conv_s = _sample_layer(
            xs, dict(lw, cos=cos_s, sin=sin_s), tabs_s, l, cache_mla, cache4, state_nsa_win[l], state_conv[l],
            pt_flat, b, nq, n_pages)
        for lst, val in zip(outs, (mla_p, mla_s, nsa_p, nsa_s, win_p, win_s, conv_p, conv_s)):
            lst.append(val)
    y_prompt = _rmsnorm(xp, final_g, F32).reshape(1, t, d)
    y_sample = _rmsnorm(xs, final_g, F32).reshape(b, nq, d)
    return (y_prompt, y_sample) + tuple(jnp.stack(o) for o in outs)
```

```python
import functools
import math

import jax
import jax.numpy as jnp
import numpy as np
from jax import lax
from jax.experimental import pallas as pl
from jax.experimental.pallas import tpu as pltpu

F32 = jnp.float32
BF16 = jnp.bfloat16
I32 = jnp.int32

HEADS = 8
DK = 128
ROPE_DIM = 32
ROPE_THETA = 10000.0
CMP_BLOCK = 64
N_SELECT = 16
WINDOW = 512
N_BUCKETS = 32
MAX_DIST = 128
N_BRANCH = 3
RMS_EPS = 1e-6
NEG = -1e30
FORCE_SCORE = 1e4
LOG2E = math.log2(math.e)
NSA_SCALE = DK ** -0.5
MLA_SCALE = (DK + ROPE_DIM) ** -0.5
NSA_W = HEADS * DK
KV_COLS = 6 * DK

LANE = 128
TQ = 128
ROWS = TQ * HEADS
MQ_HEAD = 3 * LANE
QPAD = 2 * LANE
VMEM_LIMIT = 56 * 1024 * 1024


def _cp(sem, vmem=VMEM_LIMIT):
    return pltpu.CompilerParams(dimension_semantics=sem, vmem_limit_bytes=vmem)


def _align(x, a):
    return -(-x // a) * a


def _dot(a, b):
    return jnp.dot(a, b, preferred_element_type=F32)


def _dot_t(a, b):
    return lax.dot_general(a, b, (((1,), (1,)), ((), ())), preferred_element_type=F32)


def _silu(x):
    return x / (1.0 + jnp.exp(-x))


def _sigmoid(x):
    return 1.0 / (1.0 + jnp.exp(-x))


def _rms_kernel(x_ref, g_ref, o_ref):
    x = x_ref[...]
    y = x * lax.rsqrt(jnp.mean(x * x, axis=-1, keepdims=True) + RMS_EPS)
    o_ref[...] = (y * g_ref[...]).astype(o_ref.dtype)


def _rmsnorm(x, g, dtype):
    m, d = x.shape
    tm = min(256, m)
    return pl.pallas_call(
        _rms_kernel, grid=(m // tm,),
        in_specs=[pl.BlockSpec((tm, d), lambda i: (i, 0)), pl.BlockSpec((1, d), lambda i: (0, 0))],
        out_specs=pl.BlockSpec((tm, d), lambda i: (i, 0)),
        out_shape=jax.ShapeDtypeStruct((m, d), dtype),
        compiler_params=_cp(("parallel",)), name="rmsnorm")(x, g.reshape(1, d))


def _mm_kernel(a_ref, b_ref, o_ref):
    o_ref[...] = _dot(a_ref[...], b_ref[...]).astype(o_ref.dtype)


def _mm_res_kernel(a_ref, b_ref, r_ref, o_ref):
    o_ref[...] = r_ref[...] + _dot(a_ref[...], b_ref[...])


def _col_tile(n, cap):
    t = min(cap, n) // LANE * LANE
    while n % t:
        t -= LANE
    return t


def _matmul(a, b, res=None, tn_cap=1280, name="matmul"):
    m, k = a.shape
    n = b.shape[1]
    tm = min(512, m)
    tn = _col_tile(n, tn_cap)
    in_specs = [pl.BlockSpec((tm, k), lambda j, i: (i, 0)), pl.BlockSpec((k, tn), lambda j, i: (0, j))]
    args = [a, b]
    kern = _mm_kernel
    if res is not None:
        in_specs.append(pl.BlockSpec((tm, tn), lambda j, i: (i, j)))
        args.append(res)
        kern = _mm_res_kernel
    return pl.pallas_call(
        kern, grid=(n // tn, m // tm), in_specs=in_specs,
        out_specs=pl.BlockSpec((tm, tn), lambda j, i: (i, j)),
        out_shape=jax.ShapeDtypeStruct((m, n), F32),
        compiler_params=_cp(("parallel", "parallel")), name=name)(*args)


def _pack_layout(c, ql, d):
    groups = [("conv_u", c, c), ("conv_b", c, c), ("conv_c", c, c), ("conv_z", c, c),
              ("nsa_q", NSA_W, NSA_W), ("nsa_z", NSA_W, NSA_W), ("mla_z", NSA_W, NSA_W),
              ("merge_g", N_BRANCH * d, min(d, 1024)), ("mla_cq", ql, ql), ("nsa_kv", KV_COLS, KV_COLS),
              ("mla_ckv", DK, LANE), ("g1", LANE, LANE), ("g2", LANE, LANE)]
    off, lay = 0, {}
    for name, w, a in groups:
        off = _align(off, a)
        lay[name] = off
        off += w
    return lay, _align(off, LANE)


def _pack_w_in(w, lay, n_pack, c, ql, d):
    src, off = {}, 0
    for name, n in (("conv_u", c), ("conv_b", c), ("conv_c", c), ("conv_z", c), ("nsa_q", NSA_W),
                    ("nsa_kv", KV_COLS), ("nsa_g", N_BRANCH * HEADS), ("nsa_z", NSA_W), ("mla_cq", ql),
                    ("mla_ckv", DK), ("mla_kr", ROPE_DIM), ("mla_z", NSA_W), ("merge_g", N_BRANCH * d)):
        src[name] = w[:, off:off + n]
        off += n
    half = ROPE_DIM // 2
    kr = src["mla_kr"]
    kr_sw = jnp.concatenate([kr[:, half:], kr[:, :half]], axis=1)
    zeros = lambda n: jnp.zeros((w.shape[0], n), w.dtype)
    dst = dict(src)
    dst["g1"] = jnp.concatenate([kr, src["nsa_g"], zeros(LANE - ROPE_DIM - N_BRANCH * HEADS)], axis=1)
    dst["g2"] = jnp.concatenate([kr_sw, zeros(LANE - ROPE_DIM)], axis=1)
    pieces, pos = [], 0
    for name, o in sorted(lay.items(), key=lambda kv: kv[1]):
        if o > pos:
            pieces.append(zeros(o - pos))
        pieces.append(dst[name])
        pos = o + dst[name].shape[1]
    if n_pack > pos:
        pieces.append(zeros(n_pack - pos))
    return jnp.concatenate(pieces, axis=1).astype(BF16)


def _pack_w_uq(w_uq):
    ql = w_uq.shape[0]
    w = w_uq.reshape(ql, HEADS, DK + ROPE_DIM)
    half = ROPE_DIM // 2
    nope, r = w[..., :DK], w[..., DK:]
    r_sw = jnp.concatenate([r[..., half:], r[..., :half]], axis=-1)
    z = jnp.zeros((ql, HEADS, LANE - ROPE_DIM), w.dtype)
    return jnp.concatenate([nope, r, z, r_sw, z], axis=-1).reshape(ql, HEADS * MQ_HEAD).astype(BF16)


def _rope_tables(pos):
    inv = ROPE_THETA ** (-jnp.arange(0, ROPE_DIM, 2, dtype=F32) / ROPE_DIM)
    ang = jnp.asarray(pos, F32)[:, None] * inv
    c, s = jnp.cos(ang), jnp.sin(ang)
    z = jnp.zeros((ang.shape[0], LANE - ROPE_DIM), F32)
    return jnp.concatenate([c, c, z], axis=1), jnp.concatenate([-s, s, z], axis=1)


def _bucket_of_dist():
    d = np.arange(MAX_DIST + 1)
    exact = N_BUCKETS // 2
    large = exact + np.floor(np.log(np.maximum(d, 1) / exact) / math.log(MAX_DIST / exact)
                             * (N_BUCKETS - exact) + 1e-9).astype(np.int64)
    return np.where(d < exact, d, np.minimum(large, N_BUCKETS - 1)).astype(np.int32)


def _bias_lookup(tbl, dist, valid=None):
    idx = np.clip(dist, 0, MAX_DIST).astype(np.int32)
    b = tbl[:, idx]
    if valid is not None:
        b = jnp.where(jnp.asarray(valid)[None], b, NEG)
    return b


def _prompt_tables(tbl):
    rk = np.arange(TQ)[:, None]
    rq = np.arange(TQ)[None, :]
    d0, d1 = rq - rk, TQ + rq - rk
    far_h = tbl[:, MAX_DIST][:, None, None]
    rel = lambda dist, valid: jnp.where(jnp.asarray(valid)[None], (_bias_lookup(tbl, dist) - far_h) * LOG2E, NEG)
    t0 = rel(d0, d0 >= 0)
    t1 = rel(d1, d1 >= 0)
    mask = lambda valid: jnp.broadcast_to(jnp.where(jnp.asarray(valid), 0.0, NEG), (HEADS, TQ, TQ)).astype(F32)
    sel_tab = jnp.stack([t1, t0])
    win_tab = jnp.stack([mask(rk > rq), t1, t0])
    mla_tab = mask(d0 >= 0)[None]
    m = np.arange(-2, 2)[None, :]
    dc = np.arange(TQ)[:, None] - CMP_BLOCK * m - (CMP_BLOCK - 1)
    near = _bias_lookup(tbl, dc).reshape(ROWS, 4)
    cmp_near = jnp.concatenate([near, jnp.zeros((ROWS, LANE - 4), F32)], axis=1)
    far = _bias_lookup(tbl, np.full((TQ, TQ), MAX_DIST)).reshape(ROWS, TQ)
    return sel_tab, win_tab, mla_tab, cmp_near, far


def _sample_tables(tbl, past, nq, nbp):
    nbs = past // CMP_BLOCK
    qi = np.arange(nq)
    qpos = past + qi
    rows = lambda b: jnp.moveaxis(b, 0, 1).reshape(nq * HEADS, -1)
    n = np.arange(nbp)
    dc = qpos[:, None] - (CMP_BLOCK * n[None, :] + CMP_BLOCK - 1)
    cmp_tab = rows(_bias_lookup(tbl, dc, (dc >= 0) & (n[None, :] < nbs)))
    i = np.arange(WINDOW + LANE)
    kpos = past - WINDOW + i
    dw = qpos[:, None] - kpos[None, :]
    win_tab = rows(_bias_lookup(tbl, dw, (dw >= 0) & (dw < WINDOW) & (i[None, :] < WINDOW + nq)))
    r = np.arange(LANE) % CMP_BLOCK
    d_last = (qpos[:, None] - (nbs - 1) * CMP_BLOCK) - r[None, :]
    d_prev = d_last + CMP_BLOCK
    j = np.arange(LANE)
    d_new = qi[:, None] - j[None, :]
    sel_tab = jnp.stack([jnp.moveaxis(_bias_lookup(tbl, d_last), 0, 1),
                         jnp.moveaxis(_bias_lookup(tbl, d_prev), 0, 1),
                         jnp.moveaxis(_bias_lookup(tbl, d_new, (d_new >= 0) & (j[None, :] < nq)), 0, 1),
                         jnp.moveaxis(_bias_lookup(tbl, np.full((nq, LANE), MAX_DIST)), 0, 1)])
    new_mask = jnp.where(jnp.asarray((d_new >= 0) & (j[None, :] < nq)), 0.0, NEG)
    mla_new = jnp.repeat(new_mask, HEADS, axis=0)
    return cmp_tab, win_tab, sel_tab, mla_new


def _pre_kernel(u_ref, c_ref, cq_ref, kv_ref, ckv_ref, g1_ref, g2_ref, gq_ref, gkv_ref, cos_ref, sin_ref,
                convv_ref, cqn_ref, mrow_ref, kpad_ref, kvb_ref):
    convv_ref[...] = c_ref[...] * u_ref[...]
    cq = cq_ref[...]
    cqn = cq * lax.rsqrt(jnp.mean(cq * cq, axis=-1, keepdims=True) + RMS_EPS) * gq_ref[...]
    cqn_ref[...] = cqn.astype(BF16)
    ckv = ckv_ref[...]
    lat = ckv * lax.rsqrt(jnp.mean(ckv * ckv, axis=-1, keepdims=True) + RMS_EPS) * gkv_ref[...]
    roped = g1_ref[...] * cos_ref[...] + g2_ref[...] * sin_ref[...]
    mrow_ref[:, :DK] = lat
    mrow_ref[:, DK:] = roped[:, :ROPE_DIM]
    kpad_ref[...] = jnp.concatenate([lat, roped], axis=1).astype(BF16)
    kvb_ref[...] = kv_ref[...].astype(BF16)


def _pre_mix(p, lay, c, ql, gq, gkv, cos_t, sin_t):
    m = p.shape[0]
    tm = min(256, m)
    col = lambda name, w: pl.BlockSpec((tm, w), functools.partial(lambda i, o: (i, o), o=lay[name] // w))
    row = lambda w: pl.BlockSpec((tm, w), lambda i: (i, 0))
    par = lambda w: pl.BlockSpec((1, w), lambda i: (0, 0))
    return pl.pallas_call(
        _pre_kernel, grid=(m // tm,),
        in_specs=[col("conv_u", c), col("conv_c", c), col("mla_cq", ql), col("nsa_kv", KV_COLS),
                  col("mla_ckv", DK), col("g1", LANE), col("g2", LANE), par(ql), par(DK), row(LANE), row(LANE)],
        out_specs=[row(c), row(ql), row(DK + ROPE_DIM), row(QPAD), row(KV_COLS)],
        out_shape=[jax.ShapeDtypeStruct((m, c), F32), jax.ShapeDtypeStruct((m, ql), BF16),
                   jax.ShapeDtypeStruct((m, DK + ROPE_DIM), F32), jax.ShapeDtypeStruct((m, QPAD), BF16),
                   jax.ShapeDtypeStruct((m, KV_COLS), BF16)],
        compiler_params=_cp(("parallel",)), name="pre_mix",
    )(p, p, p, p, p, p, p, gq.reshape(1, ql), gkv.reshape(1, DK), cos_t, sin_t)


def _mlaq_kernel(mq_ref, wk_ref, cos_ref, sin_ref, q_ref):
    cos_t, sin_t = cos_ref[...], sin_ref[...]
    for h in range(HEADS):
        base = h * MQ_HEAD
        q_lat = _dot(mq_ref[:, base:base + DK].astype(BF16), wk_ref[h])
        roped = mq_ref[:, base + LANE:base + 2 * LANE] * cos_t + mq_ref[:, base + 2 * LANE:base + 3 * LANE] * sin_t
        q_ref[:, h * QPAD:h * QPAD + DK] = q_lat.astype(q_ref.dtype)
        q_ref[:, h * QPAD + DK:(h + 1) * QPAD] = roped.astype(q_ref.dtype)


def _mla_query(mq, wk, cos_t, sin_t, dtype):
    m = mq.shape[0]
    tm = min(256, m)
    return pl.pallas_call(
        _mlaq_kernel, grid=(m // tm,),
        in_specs=[pl.BlockSpec((tm, HEADS * MQ_HEAD), lambda i: (i, 0)),
                  pl.BlockSpec((HEADS, DK, DK), lambda i: (0, 0, 0)),
                  pl.BlockSpec((tm, LANE), lambda i: (i, 0)), pl.BlockSpec((tm, LANE), lambda i: (i, 0))],
        out_specs=pl.BlockSpec((tm, HEADS * QPAD), lambda i: (i, 0)),
        out_shape=jax.ShapeDtypeStruct((m, HEADS * QPAD), dtype),
        compiler_params=_cp(("parallel",)), name="mla_query")(mq, wk, cos_t, sin_t)


def _compress_rows(rows_ref, x_scr, pos_ref, w1_ref, w2_ref, j):
    nb = x_scr.shape[0]
    for c in range(CMP_BLOCK):
        xc = rows_ref[pl.ds(c, nb, stride=CMP_BLOCK), :] + pos_ref[j, c:c + 1, :]
        x_scr[:, c * DK:(c + 1) * DK] = xc.astype(BF16)
    hid = _dot(x_scr[...], w1_ref[j])
    return _dot(_silu(hid).astype(BF16), w2_ref[j])


def _cmp_prompt_kernel(rows_ref, pos_ref, w1_ref, w2_ref, o_ref, x_scr):
    o_ref[0] = _compress_rows(rows_ref, x_scr, pos_ref, w1_ref, w2_ref, 0).astype(o_ref.dtype)


def _compress_prompt(p, lay, pos, w1, w2):
    t = p.shape[0]
    nb = t // CMP_BLOCK
    hid = w1.shape[-1]
    base = lay["nsa_kv"] // DK
    return pl.pallas_call(
        _cmp_prompt_kernel, grid=(2,),
        in_specs=[pl.BlockSpec((t, DK), lambda j: (0, base + j)),
                  pl.BlockSpec((1, CMP_BLOCK, DK), lambda j: (j, 0, 0)),
                  pl.BlockSpec((1, CMP_BLOCK * DK, hid), lambda j: (j, 0, 0)),
                  pl.BlockSpec((1, hid, DK), lambda j: (j, 0, 0))],
        out_specs=pl.BlockSpec((1, nb, DK), lambda j: (j, 0, 0)),
        out_shape=jax.ShapeDtypeStruct((2, nb, DK), BF16),
        scratch_shapes=[pltpu.VMEM((nb, CMP_BLOCK * DK), BF16)],
        compiler_params=_cp(("parallel",)), name="compress_prompt")(p, pos, w1, w2)


def _plane_copies(cache_ref, layer, pt_ref, pt_base, n_pages, plane, dst, sem):
    page = cache_ref.shape[2]
    return [pltpu.make_async_copy(cache_ref.at[layer, pt_ref[pt_base + pg], :, plane, :],
                                  dst.at[pl.ds(pg * page, page), :], sem) for pg in range(n_pages)]


def _cmp_sample_kernel(pt_ref, pos_ref, w1_ref, w2_ref, cache_ref, kc_ref, vc_ref, buf, sem, x_scr, *, layer, pps):
    s = pl.program_id(0)
    n = pl.num_programs(0)
    slot = s % 2
    fetch = lambda step, sl: [cp for j in range(2) for cp in _plane_copies(
        cache_ref, layer, pt_ref, step * pps, pps, j, buf.at[sl, j], sem.at[sl])]

    @pl.when(s == 0)
    def _():
        for cp in fetch(0, 0):
            cp.start()

    @pl.when(s + 1 < n)
    def _():
        for cp in fetch(s + 1, 1 - slot):
            cp.start()

    for cp in fetch(s, slot):
        cp.wait()
    kc_ref[...] = _compress_rows(buf.at[slot, 0], x_scr, pos_ref, w1_ref, w2_ref, 0).astype(kc_ref.dtype)
    vc_ref[...] = _compress_rows(buf.at[slot, 1], x_scr, pos_ref, w1_ref, w2_ref, 1).astype(vc_ref.dtype)


def _compress_sample(pt_flat, cache_nsa, layer, pos, w1, w2, pps):
    page = cache_nsa.shape[2]
    n_tot = pt_flat.shape[0]
    nb = pps * page // CMP_BLOCK
    hid = w1.shape[-1]
    full = lambda shape: pl.BlockSpec(shape, lambda s, pt: (0,) * len(shape))
    grid_spec = pltpu.PrefetchScalarGridSpec(
        num_scalar_prefetch=1, grid=(n_tot // pps,),
        in_specs=[full((2, CMP_BLOCK, DK)), full((2, CMP_BLOCK * DK, hid)), full((2, hid, DK)),
                  pl.BlockSpec(memory_space=pl.ANY)],
        out_specs=[pl.BlockSpec((nb, DK), lambda s, pt: (s, 0)), pl.BlockSpec((nb, DK), lambda s, pt: (s, 0))],
        scratch_shapes=[pltpu.VMEM((2, 2, pps * page, DK), F32), pltpu.SemaphoreType.DMA((2,)),
                        pltpu.VMEM((nb, CMP_BLOCK * DK), BF16)])
    n_blocks = n_tot * page // CMP_BLOCK
    return pl.pallas_call(
        functools.partial(_cmp_sample_kernel, layer=layer, pps=pps), grid_spec=grid_spec,
        out_shape=[jax.ShapeDtypeStruct((n_blocks, DK), BF16), jax.ShapeDtypeStruct((n_blocks, DK), BF16)],
        compiler_params=_cp(("arbitrary",)), name="compress_sample")(pt_flat, pos, w1, w2, cache_nsa)


def _select_blocks(score, blk):
    sel = jnp.zeros(score.shape, F32)
    picks = []
    for _ in range(N_SELECT):
        top = jnp.max(score, axis=-1, keepdims=True)
        pick = jnp.min(jnp.where(score == top, blk, jnp.int32(1 << 30)), axis=-1, keepdims=True)
        hit = blk == pick
        sel = jnp.where(hit & (top > NEG / 2), 1.0, sel)
        score = jnp.where(hit, -3e38, score)
        picks.append(pick)
    return sel, picks


def _block_scores(imp, blk, cur):
    forced = (blk == 0) | (blk == cur) | (blk == cur - 1)
    return jnp.where(blk > cur, NEG, jnp.where(forced, FORCE_SCORE, imp))


def _cmp_attn_prompt_kernel(q_ref, kc_ref, vc_ref, near_ref, far_ref, o_ref, sel_ref):
    qt = pl.program_id(0)
    nbp = kc_ref.shape[0]
    q = jnp.concatenate([q_ref[:, h * DK:(h + 1) * DK].astype(BF16) for h in range(HEADS)], axis=0)
    s = _dot_t(q, kc_ref[...]) * NSA_SCALE
    lane = lax.broadcasted_iota(I32, (ROWS, nbp), 1)
    qpos = qt * TQ + (lax.broadcasted_iota(I32, (ROWS, nbp), 0) & (TQ - 1))
    bias = jnp.broadcast_to(far_ref[:, 0:1], (ROWS, nbp))
    for i in range(4):
        bias = jnp.where(lane == 2 * qt + (i - 2), near_ref[:, i:i + 1], bias)
    valid = CMP_BLOCK * lane + (CMP_BLOCK - 1) <= qpos
    s = jnp.where(valid, s + bias, NEG)
    e = jnp.where(valid, jnp.exp(s - jnp.max(s, axis=-1, keepdims=True)), 0.0)
    den = jnp.sum(e, axis=-1, keepdims=True)
    p = e / jnp.where(den > 0.0, den, 1.0)
    o = _dot(p.astype(BF16), vc_ref[...])
    imp = p[0:TQ]
    for h in range(1, HEADS):
        imp = imp + p[h * TQ:(h + 1) * TQ]
    for h in range(HEADS):
        o_ref[:, h * DK:(h + 1) * DK] = o[h * TQ:(h + 1) * TQ]
    blk = lax.broadcasted_iota(I32, (TQ, nbp), 1)
    cur = (qt * TQ + lax.broadcasted_iota(I32, (TQ, nbp), 0)) // CMP_BLOCK
    sel, _ = _select_blocks(_block_scores(imp, blk, cur), blk)
    sel_ref[...] = sel


def _cmp_attn_prompt(p, lay, kc, vc, near, far):
    t = p.shape[0]
    nbp = kc.shape[0]
    full = lambda a: pl.BlockSpec(a.shape, lambda i: (0,) * a.ndim)
    return pl.pallas_call(
        _cmp_attn_prompt_kernel, grid=(t // TQ,),
        in_specs=[pl.BlockSpec((TQ, NSA_W), functools.partial(lambda i, o: (i, o), o=lay["nsa_q"] // NSA_W)),
                  full(kc), full(vc), full(near), full(far)],
        out_specs=[pl.BlockSpec((TQ, NSA_W), lambda i: (i, 0)), pl.BlockSpec((TQ, nbp), lambda i: (i, 0))],
        out_shape=[jax.ShapeDtypeStruct((t, NSA_W), F32), jax.ShapeDtypeStruct((t, nbp), F32)],
        compiler_params=_cp(("parallel",)), name="cmp_attn_prompt")(p, kc, vc, near, far)


def _attn_prompt_kernel(*refs, dq, scale, mode):
    if mode == "sel":
        q_ref, k_ref, vt_ref, tab_ref, sel_ref, o_ref, q_scr, m_scr, l_scr, acc_scr = refs
    else:
        q_ref, k_ref, vt_ref, tab_ref, o_ref, q_scr, m_scr, l_scr, acc_scr = refs
    qt = pl.program_id(0)
    for h in range(HEADS):
        qh = q_ref[:, h * dq:(h + 1) * dq].astype(BF16)
        if mode == "sel":
            qh = jnp.concatenate([qh, ((sel_ref[...] - 1.0) * -NEG).astype(BF16)], axis=1)
        q_scr[h * TQ:(h + 1) * TQ, :] = qh
    m_scr[...] = jnp.full(m_scr.shape, NEG, F32)
    l_scr[...] = jnp.zeros(l_scr.shape, F32)
    acc_scr[...] = jnp.zeros(acc_scr.shape, F32)
    scale2 = scale * LOG2E

    last_tile = k_ref.shape[0] // TQ - 1
    clamp = lambda kt: jnp.clip(kt, 0, last_tile)

    def logits_of(kt):
        row = pl.multiple_of(kt * TQ, TQ)
        k = k_ref[pl.ds(row, TQ), :]
        if mode == "sel":
            nbp = sel_ref.shape[1]
            key = lax.broadcasted_iota(I32, (TQ, nbp), 0)
            blk = lax.broadcasted_iota(I32, (TQ, nbp), 1)
            expand = jnp.where(blk == 2 * kt + key // CMP_BLOCK, 1.0, 0.0).astype(BF16)
            k = jnp.concatenate([k, expand], axis=1)
        return _dot_t(k, q_scr[...])

    def softmax_update(logits, tab):
        weights, alphas = [], []
        for h in range(HEADS):
            s = logits[:, h * TQ:(h + 1) * TQ] * scale2
            if tab is not None:
                s = s + tab_ref[tab, h]
            m_prev = m_scr[h:h + 1, :]
            m_new = jnp.maximum(m_prev, jnp.max(s, axis=0, keepdims=True))
            alpha = jnp.exp2(m_prev - m_new)
            e = jnp.exp2(s - m_new)
            l_scr[h:h + 1, :] = alpha * l_scr[h:h + 1, :] + jnp.sum(e, axis=0, keepdims=True)
            m_scr[h:h + 1, :] = m_new
            weights.append(e.astype(BF16))
            alphas.append(alpha)
        return jnp.concatenate(weights, axis=1), jnp.concatenate(alphas, axis=1)

    def accumulate(kt, weights, alpha):
        acc_scr[...] = alpha * acc_scr[...] + _dot(vt_ref[kt], weights)

    def pipelined(lo, hi):
        def body(kt, carry):
            logits, w_prev, a_prev = carry
            nxt = logits_of(clamp(kt + 1))
            pv = _dot(vt_ref[clamp(kt - 1)], w_prev)
            w, a = softmax_update(logits, None)
            acc_scr[...] = a_prev * acc_scr[...] + pv
            return nxt, w, a

        init = (logits_of(clamp(lo)), jnp.zeros((TQ, ROWS), BF16), jnp.ones((1, ROWS), F32))
        _, w, a = lax.fori_loop(lo, hi, body, init)
        accumulate(clamp(hi - 1), w, a)

    def special(delta, tab):
        @pl.when(qt >= delta)
        def _():
            w, a = softmax_update(logits_of(qt - delta), tab)
            accumulate(qt - delta, w, a)

    n_win = WINDOW // TQ
    if mode == "mla":
        pipelined(0, qt)
        special(0, 0)
    elif mode == "sel":
        pipelined(0, jnp.maximum(qt - 1, 0))
        special(1, 0)
        special(0, 1)
    else:
        special(n_win, 0)
        pipelined(jnp.maximum(qt - n_win + 1, 0), jnp.maximum(qt - 1, 0))
        special(1, 1)
        special(0, 2)
    for h in range(HEADS):
        o_ref[:, h * DK:(h + 1) * DK] = (acc_scr[:, h * TQ:(h + 1) * TQ] / l_scr[h:h + 1, :]).T


def _key_major(v):
    return jnp.transpose(v.reshape(v.shape[0] // TQ, TQ, DK), (0, 2, 1))


def _attn_prompt(q, q_col, dq, k, k_col, vt, tab, scale, mode, sel=None):
    t = k.shape[0]
    full = lambda a: pl.BlockSpec(a.shape, lambda i: (0,) * a.ndim)
    in_specs = [pl.BlockSpec((TQ, HEADS * dq), functools.partial(lambda i, o: (i, o), o=q_col)),
                pl.BlockSpec((t, dq), functools.partial(lambda i, o: (0, o), o=k_col)),
                full(vt), full(tab)]
    args = [q, k, vt, tab]
    dqa = dq
    if mode == "sel":
        in_specs.append(pl.BlockSpec((TQ, sel.shape[1]), lambda i: (i, 0)))
        args.append(sel)
        dqa = dq + sel.shape[1]
    return pl.pallas_call(
        functools.partial(_attn_prompt_kernel, dq=dq, scale=scale, mode=mode), grid=(t // TQ,),
        in_specs=in_specs,
        out_specs=pl.BlockSpec((TQ, NSA_W), lambda i: (i, 0)),
        out_shape=jax.ShapeDtypeStruct((t, NSA_W), F32),
        scratch_shapes=[pltpu.VMEM((ROWS, dqa), BF16), pltpu.VMEM((HEADS, TQ), F32),
                        pltpu.VMEM((HEADS, TQ), F32), pltpu.VMEM((DK, ROWS), F32)],
        compiler_params=_cp(("parallel",)), name="attn_prompt_" + mode)(*args)


def _softmax_rows(s):
    e = jnp.exp(s - jnp.max(s, axis=-1, keepdims=True))
    return e / jnp.sum(e, axis=-1, keepdims=True)


def _cmp_attn_sample_kernel(q_ref, kc_ref, vc_ref, tab_ref, o_ref, idx_ref, imp_scr, *, nbs):
    gb, nq, width = imp_scr.shape
    nbp = kc_ref.shape[0] // gb
    imp_scr[...] = jnp.zeros(imp_scr.shape, F32)
    for g in range(gb):
        q = q_ref[g * nq:(g + 1) * nq].reshape(nq * HEADS, DK).astype(BF16)
        kc = kc_ref[g * nbp:(g + 1) * nbp, :]
        p = _softmax_rows(_dot_t(q, kc) * NSA_SCALE + tab_ref[...])
        o_ref[g * nq:(g + 1) * nq] = _dot(p.astype(BF16), vc_ref[g * nbp:(g + 1) * nbp, :]).reshape(nq, HEADS, DK)
        imp_scr[g, :, 0:nbp] = jnp.sum(p.reshape(nq, HEADS, nbp), axis=1)
    blk = lax.broadcasted_iota(I32, imp_scr.shape, 2)
    imp = jnp.where(blk < nbs, imp_scr[...], 0.0)
    _, picks = _select_blocks(_block_scores(imp, blk, nbs), blk)
    lane = lax.broadcasted_iota(I32, idx_ref.shape, 2)
    out = jnp.zeros(idx_ref.shape, I32)
    for i, pick in enumerate(picks):
        out = jnp.where(lane == i, pick, out)
    idx_ref[...] = out


def _cmp_attn_sample(q3, kc, vc, tab, nq, nbs):
    ms = q3.shape[0]
    b = ms // nq
    nbp = kc.shape[0] // b
    gb = math.gcd(b, 8)
    return pl.pallas_call(
        functools.partial(_cmp_attn_sample_kernel, nbs=nbs), grid=(b // gb,),
        in_specs=[pl.BlockSpec((gb * nq, HEADS, DK), lambda i: (i, 0, 0)),
                  pl.BlockSpec((gb * nbp, DK), lambda i: (i, 0)), pl.BlockSpec((gb * nbp, DK), lambda i: (i, 0)),
                  pl.BlockSpec(tab.shape, lambda i: (0, 0))],
        out_specs=[pl.BlockSpec((gb * nq, HEADS, DK), lambda i: (i, 0, 0)),
                   pl.BlockSpec((gb, nq, LANE), lambda i: (i, 0, 0))],
        out_shape=[jax.ShapeDtypeStruct((ms, HEADS, DK), F32), jax.ShapeDtypeStruct((b, nq, LANE), I32)],
        scratch_shapes=[pltpu.VMEM((gb, nq, _align(nbs + 1, LANE)), F32)],
        compiler_params=_cp(("parallel",)), name="cmp_attn_sample")(q3, kc, vc, tab)


def _pad_new_rows(pad_scr, new):
    pad_scr[...] = jnp.zeros(pad_scr.shape, F32)
    pad_scr[0:new.shape[0], :] = new
    return pad_scr[...]


def _win_sample_kernel(q_ref, st_ref, new_ref, tab_ref, o_ref, pad_scr):
    nq = q_ref.shape[0]
    q = q_ref[...].reshape(nq * HEADS, DK).astype(BF16)
    st = st_ref.at[0]
    k_st = st[pl.ds(0, WINDOW, stride=2), :].astype(BF16)
    v_st = st[pl.ds(1, WINDOW, stride=2), :].astype(BF16)
    newp = _pad_new_rows(pad_scr, new_ref[0])
    s = jnp.concatenate([_dot_t(q, k_st), _dot_t(q, newp[:, :DK].astype(BF16))], axis=1)
    p = _softmax_rows(s * NSA_SCALE + tab_ref[...]).astype(BF16)
    o = _dot(p[:, :WINDOW], v_st) + _dot(p[:, WINDOW:], newp[:, DK:].astype(BF16))
    o_ref[...] = o.reshape(nq, HEADS, DK)


def _win_sample(q3, state, new, tab, nq):
    ms = q3.shape[0]
    return pl.pallas_call(
        _win_sample_kernel, grid=(ms // nq,),
        in_specs=[pl.BlockSpec((nq, HEADS, DK), lambda i: (i, 0, 0)),
                  pl.BlockSpec((1, 2 * WINDOW, DK), lambda i: (i, 0, 0)),
                  pl.BlockSpec((1, nq, 2 * DK), lambda i: (i, 0, 0)),
                  pl.BlockSpec(tab.shape, lambda i: (0, 0))],
        out_specs=pl.BlockSpec((nq, HEADS, DK), lambda i: (i, 0, 0)),
        out_shape=jax.ShapeDtypeStruct((ms, HEADS, DK), F32),
        scratch_shapes=[pltpu.VMEM((LANE, 2 * DK), F32)],
        compiler_params=_cp(("parallel",)), name="win_sample")(q3, state, new, tab)


def _sel_copies(cache_ref, layer, idx_ref, pt_ref, step, nq, n_pages, nbs, buf, slot, sem):
    per_page = cache_ref.shape[2] // CMP_BLOCK
    copies = []
    for i in range(nq * N_SELECT):
        j = jnp.minimum(idx_ref[step * nq * N_SELECT + i], nbs - 1)
        page = pt_ref[step * n_pages + j // per_page]
        r0 = pl.multiple_of((j % per_page) * CMP_BLOCK, CMP_BLOCK)
        for plane in range(2):
            src = cache_ref.at[layer, page, pl.ds(r0, CMP_BLOCK), 2 + plane, :]
            dst = buf.at[slot, plane, pl.ds(i * CMP_BLOCK, CMP_BLOCK), :]
            copies.append(pltpu.make_async_copy(src, dst, sem.at[slot]))
    return copies


def _sel_sample_kernel(idx_ref, pt_ref, q_ref, new_ref, tab_ref, cache_ref, o_ref, buf, sem, pad_scr,
                       *, layer, n_pages, nbs):
    s = pl.program_id(0)
    n = pl.num_programs(0)
    nq = q_ref.shape[0]
    slot = s % 2
    fetch = lambda step, sl: _sel_copies(cache_ref, layer, idx_ref, pt_ref, step, nq, n_pages, nbs, buf, sl, sem)

    @pl.when(s == 0)
    def _():
        for cp in fetch(0, 0):
            cp.start()

    @pl.when(s + 1 < n)
    def _():
        for cp in fetch(s + 1, 1 - slot):
            cp.start()

    for cp in fetch(s, slot):
        cp.wait()
    newp = _pad_new_rows(pad_scr, new_ref[0])
    kn, vn = newp[:, :DK].astype(BF16), newp[:, DK:].astype(BF16)
    lane = lax.broadcasted_iota(I32, (HEADS, LANE), 1)
    span = N_SELECT * CMP_BLOCK
    for qi in range(nq):
        q = q_ref[qi].astype(BF16)
        k = buf[slot, 0, qi * span:(qi + 1) * span, :].astype(BF16)
        v = buf[slot, 1, qi * span:(qi + 1) * span, :].astype(BF16)

        def block_bias(i):
            j = idx_ref[(s * nq + qi) * N_SELECT + i]
            near = jnp.where(j == nbs - 1, tab_ref[0, qi], jnp.where(j == nbs - 2, tab_ref[1, qi], tab_ref[3, qi]))
            return jnp.where(j >= nbs, NEG, near)

        bias = jnp.concatenate([jnp.where(lane < CMP_BLOCK, block_bias(2 * g), block_bias(2 * g + 1))
                                for g in range(N_SELECT // 2)], axis=1)
        sc = jnp.concatenate([_dot_t(q, k) * NSA_SCALE + bias, _dot_t(q, kn) * NSA_SCALE + tab_ref[2, qi]], axis=1)
        p = _softmax_rows(sc).astype(BF16)
        o_ref[qi] = _dot(p[:, :span], v) + _dot(p[:, span:], vn)


def _sel_sample(idx_flat, pt_flat, q3, new, tab, cache_nsa, layer, nq, n_pages, nbs):
    ms = q3.shape[0]
    grid_spec = pltpu.PrefetchScalarGridSpec(
        num_scalar_prefetch=2, grid=(ms // nq,),
        in_specs=[pl.BlockSpec((nq, HEADS, DK), lambda i, a, b: (i, 0, 0)),
                  pl.BlockSpec((1, nq, 2 * DK), lambda i, a, b: (i, 0, 0)),
                  pl.BlockSpec(tab.shape, lambda i, a, b: (0, 0, 0, 0)),
                  pl.BlockSpec(memory_space=pl.ANY)],
        out_specs=pl.BlockSpec((nq, HEADS, DK), lambda i, a, b: (i, 0, 0)),
        scratch_shapes=[pltpu.VMEM((2, 2, nq * N_SELECT * CMP_BLOCK, DK), F32), pltpu.SemaphoreType.DMA((2,)),
                        pltpu.VMEM((LANE, 2 * DK), F32)])
    return pl.pallas_call(
        functools.partial(_sel_sample_kernel, layer=layer, n_pages=n_pages, nbs=nbs), grid_spec=grid_spec,
        out_shape=jax.ShapeDtypeStruct((ms, HEADS, DK), F32),
        compiler_params=_cp(("arbitrary",)), name="sel_sample")(idx_flat, pt_flat, q3, new, tab, cache_nsa)


def _mla_sample_kernel(pt_ref, q_ref, new_ref, tab_ref, cache_ref, o_ref, buf, sem, pad_scr, m_scr, l_scr, acc_scr,
                       *, layer, pps, n_chunk, group):
    s = pl.program_id(0)
    n = pl.num_programs(0)
    nq = q_ref.shape[0]
    slot = s % 2
    ch = s % n_chunk
    width = cache_ref.shape[2]
    fetch = lambda step, sl: [pltpu.make_async_copy(cache_ref.at[layer, pt_ref[step * pps + pg]], buf.at[sl, pg],
                                                    sem.at[sl]) for pg in range(pps)]

    @pl.when(s == 0)
    def _():
        for cp in fetch(0, 0):
            cp.start()

    @pl.when(s + 1 < n)
    def _():
        for cp in fetch(s + 1, 1 - slot):
            cp.start()

    for cp in fetch(s, slot):
        cp.wait()

    @pl.when(ch == 0)
    def _():
        m_scr[...] = jnp.full(m_scr.shape, NEG, F32)
        l_scr[...] = jnp.zeros(l_scr.shape, F32)
        acc_scr[...] = jnp.zeros(acc_scr.shape, F32)

    q = q_ref[...].reshape(nq * HEADS, QPAD).astype(BF16)
    q_feat = q[:, :width]
    scale2 = MLA_SCALE * LOG2E

    def update(sc, weighted_values):
        m_prev = m_scr[...]
        m_new = jnp.maximum(m_prev, jnp.max(sc, axis=-1, keepdims=True))
        alpha = jnp.exp2(m_prev - m_new)
        e = jnp.exp2(sc - m_new)
        l_scr[...] = alpha * l_scr[...] + jnp.sum(e, axis=-1, keepdims=True)
        acc_scr[...] = alpha * acc_scr[...] + weighted_values(e.astype(BF16))
        m_scr[...] = m_new

    def body(g, carry):
        pages = [buf[slot, g * group + i].astype(BF16) for i in range(group)]
        sc = jnp.concatenate([_dot(q_feat, pg) for pg in pages], axis=1) * scale2
        page_len = pages[0].shape[1]

        def weighted_values(e):
            out = _dot_t(e[:, 0:page_len], pages[0][:DK, :])
            for i in range(1, group):
                out = out + _dot_t(e[:, i * page_len:(i + 1) * page_len], pages[i][:DK, :])
            return out

        update(sc, weighted_values)
        return carry

    lax.fori_loop(0, pps // group, body, 0)

    @pl.when(ch == n_chunk - 1)
    def _():
        newp = _pad_new_rows(pad_scr, new_ref[0]).astype(BF16)
        update(_dot_t(q, newp) * scale2 + tab_ref[...], lambda e: _dot(e, newp[:, :DK]))
        o_ref[...] = (acc_scr[...] / l_scr[...]).reshape(nq, HEADS, DK)


def _mla_sample(pt_flat, q3, new, tab, cache_t, layer, nq, n_pages, pps):
    ms = q3.shape[0]
    b = ms // nq
    width, page = cache_t.shape[2], cache_t.shape[3]
    n_chunk = n_pages // pps
    group = pps
    grid_spec = pltpu.PrefetchScalarGridSpec(
        num_scalar_prefetch=1, grid=(b * n_chunk,),
        in_specs=[pl.BlockSpec((nq, HEADS, QPAD), lambda s, pt: (s // n_chunk, 0, 0)),
                  pl.BlockSpec((1, nq, QPAD), lambda s, pt: (s // n_chunk, 0, 0)),
                  pl.BlockSpec(tab.shape, lambda s, pt: (0, 0)),
                  pl.BlockSpec(memory_space=pl.ANY)],
        out_specs=pl.BlockSpec((nq, HEADS, DK), lambda s, pt: (s // n_chunk, 0, 0)),
        scratch_shapes=[pltpu.VMEM((2, pps, width, page), F32), pltpu.SemaphoreType.DMA((2,)),
                        pltpu.VMEM((LANE, QPAD), F32), pltpu.VMEM((nq * HEADS, 1), F32),
                        pltpu.VMEM((nq * HEADS, 1), F32), pltpu.VMEM((nq * HEADS, DK), F32)])
    return pl.pallas_call(
        functools.partial(_mla_sample_kernel, layer=layer, pps=pps, n_chunk=n_chunk, group=group),
        grid_spec=grid_spec, out_shape=jax.ShapeDtypeStruct((ms, HEADS, DK), F32),
        compiler_params=_cp(("arbitrary",)), name="mla_sample")(pt_flat, q3, new, tab, cache_t)


def _post_kernel(oc_ref, os_ref, ow_ref, g1_ref, nz_ref, mz_ref, lat_ref, wv_ref, v0_ref, v1_ref, v2_ref,
                 cb_ref, cz_ref, cw_ref, brc_ref, brn_ref, brm_ref):
    g = _sigmoid(g1_ref[...])
    for h in range(HEADS):
        hs = slice(h * DK, (h + 1) * DK)
        gate = lambda br: g[:, ROPE_DIM + br * HEADS + h:ROPE_DIM + br * HEADS + h + 1]
        o = gate(0) * oc_ref[:, hs] + gate(1) * os_ref[:, hs] + gate(2) * ow_ref[:, hs]
        brn_ref[:, hs] = (o * _silu(nz_ref[:, hs])).astype(BF16)
        mo = _dot(lat_ref[:, hs].astype(BF16), wv_ref[h])
        brm_ref[:, hs] = (mo * _silu(mz_ref[:, hs])).astype(BF16)
    y = cw_ref[0:1, :] * v0_ref[...] + cw_ref[1:2, :] * v1_ref[...] + cw_ref[2:3, :] * v2_ref[...]
    brc_ref[...] = (cb_ref[...] * y * _silu(cz_ref[...])).astype(BF16)


def _post_mix(o_c, o_s, o_w, p, lay, lat, wv, v0, v1, v2, conv_w, c):
    m = p.shape[0]
    tm = min(256, m)
    col = lambda name, w: pl.BlockSpec((tm, w), functools.partial(lambda i, o: (i, o), o=lay[name] // w))
    row = lambda w: pl.BlockSpec((tm, w), lambda i: (i, 0))
    return pl.pallas_call(
        _post_kernel, grid=(m // tm,),
        in_specs=[row(NSA_W), row(NSA_W), row(NSA_W), col("g1", LANE), col("nsa_z", NSA_W), col("mla_z", NSA_W),
                  row(NSA_W), pl.BlockSpec((HEADS, DK, DK), lambda i: (0, 0, 0)), row(c), row(c), row(c),
                  col("conv_b", c), col("conv_z", c), pl.BlockSpec(conv_w.shape, lambda i: (0, 0))],
        out_specs=[row(c), row(NSA_W), row(NSA_W)],
        out_shape=[jax.ShapeDtypeStruct((m, c), BF16), jax.ShapeDtypeStruct((m, NSA_W), BF16),
                   jax.ShapeDtypeStruct((m, NSA_W), BF16)],
        compiler_params=_cp(("parallel",)), name="post_mix",
    )(o_c, o_s, o_w, p, p, p, lat, wv, v0, v1, v2, p, p, conv_w)


def _merge_kernel(brc_ref, brn_ref, brm_ref, wc_ref, wn_ref, wm_ref, g0_ref, g1_ref, g2_ref, o_ref):
    merged = (_sigmoid(g0_ref[...]) * _dot(brc_ref[...], wc_ref[...])
              + _sigmoid(g1_ref[...]) * _dot(brn_ref[...], wn_ref[...])
              + _sigmoid(g2_ref[...]) * _dot(brm_ref[...], wm_ref[...]))
    o_ref[...] = merged.astype(BF16)


def _merge(brc, brn, brm, wc, wn, wm, p, lay, d):
    m = p.shape[0]
    tm = min(512, m)
    tn = _col_tile(d, 512)
    act = lambda a: pl.BlockSpec((tm, a.shape[1]), lambda j, i: (i, 0))
    wgt = lambda a: pl.BlockSpec((a.shape[0], tn), lambda j, i: (0, j))
    gate = lambda br: pl.BlockSpec((tm, tn), functools.partial(lambda j, i, o: (i, o + j), o=(lay["merge_g"] + br * d) // tn))
    return pl.pallas_call(
        _merge_kernel, grid=(d // tn, m // tm),
        in_specs=[act(brc), act(brn), act(brm), wgt(wc), wgt(wn), wgt(wm), gate(0), gate(1), gate(2)],
        out_specs=pl.BlockSpec((tm, tn), lambda j, i: (i, j)),
        out_shape=jax.ShapeDtypeStruct((m, d), BF16),
        compiler_params=_cp(("parallel", "parallel")), name="merge")(brc, brn, brm, wc, wn, wm, p, p, p)


def _project(x, lw):
    h = _rmsnorm(x, lw["norm_g"], BF16)
    p = _matmul(h, lw["w_in"], name="in_proj")
    conv_v, cqn, mrow, kpad, kvb = _pre_mix(p, lw["lay"], lw["c"], lw["ql"], lw["g_q"], lw["g_kv"], lw["cos"], lw["sin"])
    mq = _matmul(cqn, lw["w_uq"], tn_cap=1024, name="q_up_proj")
    return p, conv_v, mrow, kpad, kvb, mq


def _finish(x, p, lw, o_c, o_s, o_w, lat, v0, v1, v2):
    brc, brn, brm = _post_mix(o_c, o_s, o_w, p, lw["lay"], lat, lw["wv"], v0, v1, v2, lw["conv_w"], lw["c"])
    merged = _merge(brc, brn, brm, lw["wc"], lw["wn"], lw["wm"], p, lw["lay"], x.shape[1])
    return _matmul(merged, lw["w_out"], res=x, tn_cap=1024, name="out_proj")


def _prompt_layer(x, lw, tabs):
    lay, c = lw["lay"], lw["c"]
    t = x.shape[0]
    p, conv_v, mrow, kpad, kvb, mq = _project(x, lw)
    q_mla = _mla_query(mq, lw["wk"], lw["cos"], lw["sin"], BF16)
    zc = lambda n: jnp.zeros((n, c), F32)
    v1 = jnp.concatenate([zc(1), conv_v[:-1]], axis=0)
    v0 = jnp.concatenate([zc(2), conv_v[:-2]], axis=0)
    sel_tab, win_tab, mla_tab, cmp_near, far = tabs
    kcv = _compress_prompt(p, lay, lw["phi_pos"], lw["phi_w1"], lw["phi_w2"])
    nbp = _align(kcv.shape[1], LANE)
    kcv = jnp.pad(kcv, ((0, 0), (0, nbp - kcv.shape[1]), (0, 0)))
    o_c, sel = _cmp_attn_prompt(p, lay, kcv[0], kcv[1], cmp_near, far)
    q_col = lay["nsa_q"] // NSA_W
    o_s = _attn_prompt(p, q_col, DK, kvb, 2, _key_major(kvb[:, 3 * DK:4 * DK]), sel_tab, NSA_SCALE, "sel", sel)
    o_w = _attn_prompt(p, q_col, DK, kvb, 4, _key_major(kvb[:, 5 * DK:6 * DK]), win_tab, NSA_SCALE, "win")
    lat = _attn_prompt(q_mla, 0, QPAD, kpad, 0, _key_major(kpad[:, :DK]), mla_tab, MLA_SCALE, "mla")
    x_new = _finish(x, p, lw, o_c, o_s, o_w, lat, v0, v1, conv_v)
    kv0 = lay["nsa_kv"]
    wb = min(WINDOW, t)
    new_nsa = p[:, kv0:kv0 + 4 * DK].reshape(1, t, 4, DK)
    new_win = p[t - wb:, kv0 + 4 * DK:kv0 + 6 * DK].reshape(1, wb, 2, DK)
    return x_new, mrow[None], new_nsa, new_win, conv_v[None, -2:]


def _sample_layer(x, lw, tabs, layer, cache_mla_t, cache_nsa, state_win, state_conv, pt_flat, b, nq, n_pages):
    lay, c = lw["lay"], lw["c"]
    ms = x.shape[0]
    page = cache_nsa.shape[2]
    past = n_pages * page
    nbs = past // CMP_BLOCK
    cmp_tab, win_tab, sel_tab, mla_new = tabs
    p, conv_v, mrow, kpad, kvb, mq = _project(x, lw)
    q_mla = _mla_query(mq, lw["wk"], lw["cos"], lw["sin"], F32).reshape(ms, HEADS, QPAD)
    vp = jnp.concatenate([state_conv, conv_v.reshape(b, nq, c)], axis=1)
    v0, v1, v2 = (vp[:, i:i + nq].reshape(ms, c) for i in range(3))
    q3 = p[:, lay["nsa_q"]:lay["nsa_q"] + NSA_W].reshape(ms, HEADS, DK)
    kv0 = lay["nsa_kv"]
    kv_new = p[:, kv0:kv0 + KV_COLS].reshape(b, nq, 6, DK)
    kc, vc = _compress_sample(pt_flat, cache_nsa, layer, lw["phi_pos"], lw["phi_w1"], lw["phi_w2"], min(64, n_pages))
    nbp = cmp_tab.shape[1]
    padb = lambda a: jnp.pad(a.reshape(b, nbs, DK), ((0, 0), (0, nbp - nbs), (0, 0))).reshape(b * nbp, DK)
    o_c, idx = _cmp_attn_sample(q3, padb(kc), padb(vc), cmp_tab, nq, nbs)
    idx_flat = idx[:, :, :N_SELECT].reshape(-1)
    o_s = _sel_sample(idx_flat, pt_flat, q3, kv_new[:, :, 2:4].reshape(b, nq, 2 * DK), sel_tab, cache_nsa, layer,
                      nq, n_pages, nbs)
    win_new = kv_new[:, :, 4:6]
    o_w = _win_sample(q3, state_win.reshape(b, 2 * WINDOW, DK), win_new.reshape(b, nq, 2 * DK), win_tab, nq)
    mrow_pad = jnp.pad(mrow, ((0, 0), (0, QPAD - mrow.shape[1]))).reshape(b, nq, QPAD)
    lat = _mla_sample(pt_flat, q_mla, mrow_pad, mla_new, cache_mla_t, layer, nq, n_pages, min(64, n_pages))
    flat = lambda a: a.reshape(ms, NSA_W)
    x_new = _finish(x, p, lw, flat(o_c), flat(o_s), flat(o_w), flat(lat), v0, v1, v2)
    new_win_state = jnp.concatenate([state_win, win_new], axis=1)[:, -WINDOW:]
    return (x_new, mrow.reshape(b, nq, -1), kv_new[:, :, 0:4], new_win_state, vp[:, -2:])


def kernel(x_prompt, x_sample, cache_mla, cache_nsa, state_nsa_win, state_conv, page_table, norm_g, w_in, conv_w,
           phi_pos, phi_w1, phi_w2, mla_q_norm, mla_kv_norm, w_uq, w_ukv, rel_bias, w_branch, w_out, final_g):
    depth = w_in.shape[0]
    bp, t, d = x_prompt.shape
    b, nq, _ = x_sample.shape
    c = conv_w.shape[-1]
    ql = mla_q_norm.shape[-1]
    hid = phi_w2.shape[2]
    n_pages = page_table.shape[1]
    page = cache_mla.shape[2]
    past = n_pages * page
    assert bp == 1 and t % TQ == 0 and t >= WINDOW and WINDOW % TQ == 0
    assert state_nsa_win.shape[2] == WINDOW and past >= WINDOW and past % CMP_BLOCK == 0
    assert (past + nq) // CMP_BLOCK == past // CMP_BLOCK and past // CMP_BLOCK >= N_SELECT
    assert page % CMP_BLOCK == 0 and nq <= 8 and w_ukv.shape[1:] == (DK, HEADS, 2 * DK)

    lay, n_pack = _pack_layout(c, ql, d)
    tbl = rel_bias[jnp.asarray(_bucket_of_dist())].T.astype(F32)
    nbs = past // CMP_BLOCK
    tabs_p = _prompt_tables(tbl)
    tabs_s = _sample_tables(tbl, past, nq, _align(nbs, LANE))
    cos_p, sin_p = _rope_tables(np.arange(t))
    cos_s, sin_s = _rope_tables(np.tile(past + np.arange(nq), b))
    cache_mla_t = jnp.swapaxes(cache_mla, 2, 3)
    pt_flat = page_table.reshape(-1)

    xp = x_prompt.reshape(t, d)
    xs = x_sample.reshape(b * nq, d)
    outs = [[] for _ in range(8)]
    for l in range(depth):
        lw = {
            "lay": lay, "c": c, "ql": ql, "norm_g": norm_g[l],
            "w_in": _pack_w_in(w_in[l], lay, n_pack, c, ql, d),
            "g_q": mla_q_norm[l], "g_kv": mla_kv_norm[l], "w_uq": _pack_w_uq(w_uq[l]),
            "wk": jnp.transpose(w_ukv[l][:, :, :DK], (1, 2, 0)).astype(BF16),
            "wv": jnp.transpose(w_ukv[l][:, :, DK:], (1, 0, 2)).astype(BF16),
            "phi_pos": phi_pos[l], "phi_w1": phi_w1[l].reshape(2, CMP_BLOCK * DK, hid).astype(BF16),
            "phi_w2": phi_w2[l].astype(BF16), "conv_w": conv_w[l],
            "wc": w_branch[l][:c].astype(BF16), "wn": w_branch[l][c:c + NSA_W].astype(BF16),
            "wm": w_branch[l][c + NSA_W:].astype(BF16), "w_out": w_out[l].astype(BF16),
        }
        xp, mla_p, nsa_p, win_p, conv_p = _prompt_layer(xp, dict(lw, cos=cos_p, sin=sin_p), tabs_p)
        xs, mla_s, nsa_s, win_s, conv_s = _sample_layer(
            xs, dict(lw, cos=cos_s, sin=sin_s), tabs_s, l, cache_mla_t, cache_nsa, state_nsa_win[l], state_conv[l],
            pt_flat, b, nq, n_pages)
        for lst, val in zip(outs, (mla_p, mla_s, nsa_p, nsa_s, win_p, win_s, conv_p, conv_s)):
            lst.append(val)
    y_prompt = _rmsnorm(xp, final_g, F32).reshape(1, t, d)
    y_sample = _rmsnorm(xs, final_g, F32).reshape(b, nq, d)
    return (y_prompt, y_sample) + tuple(jnp.stack(o) for o in outs)
```

```python
import functools
import math

import jax
import jax.numpy as jnp
import numpy as np
from jax import lax
from jax.experimental import pallas as pl
from jax.experimental.pallas import tpu as pltpu

F32 = jnp.float32
BF16 = jnp.bfloat16
I32 = jnp.int32

HEADS = 8
DK = 128
ROPE_DIM = 32
ROPE_THETA = 10000.0
CMP_BLOCK = 64
N_SELECT = 16
WINDOW = 512
N_BUCKETS = 32
MAX_DIST = 128
N_BRANCH = 3
RMS_EPS = 1e-6
NEG = -1e30
FORCE_SCORE = 1e4
LOG2E = math.log2(math.e)
NSA_SCALE = DK ** -0.5
MLA_SCALE = (DK + ROPE_DIM) ** -0.5
NSA_W = HEADS * DK
KV_COLS = 6 * DK

LANE = 128
TQ = 128
ROWS = TQ * HEADS
UNROLL = 4
MQ_HEAD = 3 * LANE
QPAD = 2 * LANE
VMEM_LIMIT = 56 * 1024 * 1024


def _cp(sem, vmem=VMEM_LIMIT):
    return pltpu.CompilerParams(dimension_semantics=sem, vmem_limit_bytes=vmem)


def _align(x, a):
    return -(-x // a) * a


def _dot(a, b):
    return jnp.dot(a, b, preferred_element_type=F32)


def _dot_t(a, b):
    return lax.dot_general(a, b, (((1,), (1,)), ((), ())), preferred_element_type=F32)


def _silu(x):
    return x / (1.0 + jnp.exp(-x))


def _sigmoid(x):
    return 1.0 / (1.0 + jnp.exp(-x))


def _rms_kernel(x_ref, g_ref, o_ref):
    x = x_ref[...]
    y = x * lax.rsqrt(jnp.mean(x * x, axis=-1, keepdims=True) + RMS_EPS)
    o_ref[...] = (y * g_ref[...]).astype(o_ref.dtype)


def _rmsnorm(x, g, dtype):
    m, d = x.shape
    tm = min(256, m)
    return pl.pallas_call(
        _rms_kernel, grid=(m // tm,),
        in_specs=[pl.BlockSpec((tm, d), lambda i: (i, 0)), pl.BlockSpec((1, d), lambda i: (0, 0))],
        out_specs=pl.BlockSpec((tm, d), lambda i: (i, 0)),
        out_shape=jax.ShapeDtypeStruct((m, d), dtype),
        compiler_params=_cp(("parallel",)), name="rmsnorm")(x, g.reshape(1, d))


def _mm_kernel(a_ref, b_ref, o_ref):
    o_ref[...] = _dot(a_ref[...], b_ref[...]).astype(o_ref.dtype)


def _mm_res_kernel(a_ref, b_ref, r_ref, o_ref):
    o_ref[...] = r_ref[...] + _dot(a_ref[...], b_ref[...])


def _col_tile(n, cap):
    t = min(cap, n) // LANE * LANE
    while n % t:
        t -= LANE
    return t


def _matmul(a, b, res=None, tn_cap=1280, name="matmul"):
    m, k = a.shape
    n = b.shape[1]
    tm = min(512, m)
    tn = _col_tile(n, tn_cap)
    in_specs = [pl.BlockSpec((tm, k), lambda j, i: (i, 0)), pl.BlockSpec((k, tn), lambda j, i: (0, j))]
    args = [a, b]
    kern = _mm_kernel
    if res is not None:
        in_specs.append(pl.BlockSpec((tm, tn), lambda j, i: (i, j)))
        args.append(res)
        kern = _mm_res_kernel
    return pl.pallas_call(
        kern, grid=(n // tn, m // tm), in_specs=in_specs,
        out_specs=pl.BlockSpec((tm, tn), lambda j, i: (i, j)),
        out_shape=jax.ShapeDtypeStruct((m, n), F32),
        compiler_params=_cp(("parallel", "parallel")), name=name)(*args)


def _pack_layout(c, ql, d):
    groups = [("conv_u", c, c), ("conv_b", c, c), ("conv_c", c, c), ("conv_z", c, c),
              ("nsa_q", NSA_W, NSA_W), ("nsa_z", NSA_W, NSA_W), ("mla_z", NSA_W, NSA_W),
              ("merge_g", N_BRANCH * d, min(d, 1024)), ("mla_cq", ql, ql), ("nsa_kv", KV_COLS, KV_COLS),
              ("mla_ckv", DK, LANE), ("g1", LANE, LANE), ("g2", LANE, LANE)]
    off, lay = 0, {}
    for name, w, a in groups:
        off = _align(off, a)
        lay[name] = off
        off += w
    return lay, _align(off, LANE)


def _pack_w_in(w, lay, n_pack, c, ql, d):
    src, off = {}, 0
    for name, n in (("conv_u", c), ("conv_b", c), ("conv_c", c), ("conv_z", c), ("nsa_q", NSA_W),
                    ("nsa_kv", KV_COLS), ("nsa_g", N_BRANCH * HEADS), ("nsa_z", NSA_W), ("mla_cq", ql),
                    ("mla_ckv", DK), ("mla_kr", ROPE_DIM), ("mla_z", NSA_W), ("merge_g", N_BRANCH * d)):
        src[name] = w[:, off:off + n]
        off += n
    half = ROPE_DIM // 2
    kr = src["mla_kr"]
    kr_sw = jnp.concatenate([kr[:, half:], kr[:, :half]], axis=1)
    zeros = lambda n: jnp.zeros((w.shape[0], n), w.dtype)
    dst = dict(src)
    dst["g1"] = jnp.concatenate([kr, src["nsa_g"], zeros(LANE - ROPE_DIM - N_BRANCH * HEADS)], axis=1)
    dst["g2"] = jnp.concatenate([kr_sw, zeros(LANE - ROPE_DIM)], axis=1)
    pieces, pos = [], 0
    for name, o in sorted(lay.items(), key=lambda kv: kv[1]):
        if o > pos:
            pieces.append(zeros(o - pos))
        pieces.append(dst[name])
        pos = o + dst[name].shape[1]
    if n_pack > pos:
        pieces.append(zeros(n_pack - pos))
    return jnp.concatenate(pieces, axis=1).astype(BF16)


def _pack_w_uq(w_uq):
    ql = w_uq.shape[0]
    w = w_uq.reshape(ql, HEADS, DK + ROPE_DIM)
    half = ROPE_DIM // 2
    nope, r = w[..., :DK], w[..., DK:]
    r_sw = jnp.concatenate([r[..., half:], r[..., :half]], axis=-1)
    z = jnp.zeros((ql, HEADS, LANE - ROPE_DIM), w.dtype)
    return jnp.concatenate([nope, r, z, r_sw, z], axis=-1).reshape(ql, HEADS * MQ_HEAD).astype(BF16)


def _rope_tables(pos):
    inv = ROPE_THETA ** (-jnp.arange(0, ROPE_DIM, 2, dtype=F32) / ROPE_DIM)
    ang = jnp.asarray(pos, F32)[:, None] * inv
    c, s = jnp.cos(ang), jnp.sin(ang)
    z = jnp.zeros((ang.shape[0], LANE - ROPE_DIM), F32)
    return jnp.concatenate([c, c, z], axis=1), jnp.concatenate([-s, s, z], axis=1)


def _bucket_of_dist():
    d = np.arange(MAX_DIST + 1)
    exact = N_BUCKETS // 2
    large = exact + np.floor(np.log(np.maximum(d, 1) / exact) / math.log(MAX_DIST / exact)
                             * (N_BUCKETS - exact) + 1e-9).astype(np.int64)
    return np.where(d < exact, d, np.minimum(large, N_BUCKETS - 1)).astype(np.int32)


def _bias_lookup(tbl, dist, valid=None):
    idx = np.clip(dist, 0, MAX_DIST).astype(np.int32)
    b = tbl[:, idx]
    if valid is not None:
        b = jnp.where(jnp.asarray(valid)[None], b, NEG)
    return b


def _prompt_tables(tbl):
    rk = np.arange(TQ)[:, None]
    rq = np.arange(TQ)[None, :]
    d0, d1 = rq - rk, TQ + rq - rk
    far_h = tbl[:, MAX_DIST][:, None, None]
    rel = lambda dist, valid: jnp.where(jnp.asarray(valid)[None], (_bias_lookup(tbl, dist) - far_h) * LOG2E, NEG)
    t0 = rel(d0, d0 >= 0)
    t1 = rel(d1, d1 >= 0)
    mask = lambda valid: jnp.broadcast_to(jnp.where(jnp.asarray(valid), 0.0, NEG), (HEADS, TQ, TQ)).astype(F32)
    sel_tab = jnp.stack([t1, t0])
    win_tab = jnp.stack([mask(rk > rq), t1, t0])
    mla_tab = mask(d0 >= 0)[None]
    m = np.arange(-2, 2)[None, :]
    dc = np.arange(TQ)[:, None] - CMP_BLOCK * m - (CMP_BLOCK - 1)
    near = _bias_lookup(tbl, dc).reshape(ROWS, 4)
    cmp_near = jnp.concatenate([near, jnp.zeros((ROWS, LANE - 4), F32)], axis=1)
    far = _bias_lookup(tbl, np.full((TQ, TQ), MAX_DIST)).reshape(ROWS, TQ)
    return sel_tab, win_tab, mla_tab, cmp_near, far


def _sample_tables(tbl, past, nq, nbp):
    nbs = past // CMP_BLOCK
    qi = np.arange(nq)
    qpos = past + qi
    rows = lambda b: jnp.moveaxis(b, 0, 1).reshape(nq * HEADS, -1)
    n = np.arange(nbp)
    dc = qpos[:, None] - (CMP_BLOCK * n[None, :] + CMP_BLOCK - 1)
    cmp_tab = rows(_bias_lookup(tbl, dc, (dc >= 0) & (n[None, :] < nbs)))
    i = np.arange(WINDOW + LANE)
    kpos = past - WINDOW + i
    dw = qpos[:, None] - kpos[None, :]
    win_tab = rows(_bias_lookup(tbl, dw, (dw >= 0) & (dw < WINDOW) & (i[None, :] < WINDOW + nq)))
    r = np.arange(LANE) % CMP_BLOCK
    d_last = (qpos[:, None] - (nbs - 1) * CMP_BLOCK) - r[None, :]
    d_prev = d_last + CMP_BLOCK
    j = np.arange(LANE)
    d_new = qi[:, None] - j[None, :]
    sel_tab = jnp.stack([jnp.moveaxis(_bias_lookup(tbl, d_last), 0, 1),
                         jnp.moveaxis(_bias_lookup(tbl, d_prev), 0, 1),
                         jnp.moveaxis(_bias_lookup(tbl, d_new, (d_new >= 0) & (j[None, :] < nq)), 0, 1),
                         jnp.moveaxis(_bias_lookup(tbl, np.full((nq, LANE), MAX_DIST)), 0, 1)])
    new_mask = jnp.where(jnp.asarray((d_new >= 0) & (j[None, :] < nq)), 0.0, NEG)
    mla_new = jnp.repeat(new_mask, HEADS, axis=0)
    return cmp_tab, win_tab, sel_tab, mla_new


def _pre_kernel(u_ref, c_ref, cq_ref, kv_ref, ckv_ref, g1_ref, g2_ref, gq_ref, gkv_ref, cos_ref, sin_ref,
                convv_ref, cqn_ref, mrow_ref, kpad_ref, kvb_ref):
    convv_ref[...] = c_ref[...] * u_ref[...]
    cq = cq_ref[...]
    cqn = cq * lax.rsqrt(jnp.mean(cq * cq, axis=-1, keepdims=True) + RMS_EPS) * gq_ref[...]
    cqn_ref[...] = cqn.astype(BF16)
    ckv = ckv_ref[...]
    lat = ckv * lax.rsqrt(jnp.mean(ckv * ckv, axis=-1, keepdims=True) + RMS_EPS) * gkv_ref[...]
    roped = g1_ref[...] * cos_ref[...] + g2_ref[...] * sin_ref[...]
    mrow_ref[:, :DK] = lat
    mrow_ref[:, DK:] = roped[:, :ROPE_DIM]
    kpad_ref[...] = jnp.concatenate([lat, roped], axis=1).astype(BF16)
    kvb_ref[...] = kv_ref[...].astype(BF16)


def _pre_mix(p, lay, c, ql, gq, gkv, cos_t, sin_t):
    m = p.shape[0]
    tm = min(256, m)
    col = lambda name, w: pl.BlockSpec((tm, w), functools.partial(lambda i, o: (i, o), o=lay[name] // w))
    row = lambda w: pl.BlockSpec((tm, w), lambda i: (i, 0))
    par = lambda w: pl.BlockSpec((1, w), lambda i: (0, 0))
    return pl.pallas_call(
        _pre_kernel, grid=(m // tm,),
        in_specs=[col("conv_u", c), col("conv_c", c), col("mla_cq", ql), col("nsa_kv", KV_COLS),
                  col("mla_ckv", DK), col("g1", LANE), col("g2", LANE), par(ql), par(DK), row(LANE), row(LANE)],
        out_specs=[row(c), row(ql), row(DK + ROPE_DIM), row(QPAD), row(KV_COLS)],
        out_shape=[jax.ShapeDtypeStruct((m, c), F32), jax.ShapeDtypeStruct((m, ql), BF16),
                   jax.ShapeDtypeStruct((m, DK + ROPE_DIM), F32), jax.ShapeDtypeStruct((m, QPAD), BF16),
                   jax.ShapeDtypeStruct((m, KV_COLS), BF16)],
        compiler_params=_cp(("parallel",)), name="pre_mix",
    )(p, p, p, p, p, p, p, gq.reshape(1, ql), gkv.reshape(1, DK), cos_t, sin_t)


def _mlaq_kernel(mq_ref, wk_ref, cos_ref, sin_ref, q_ref):
    cos_t, sin_t = cos_ref[...], sin_ref[...]
    for h in range(HEADS):
        base = h * MQ_HEAD
        q_lat = _dot(mq_ref[:, base:base + DK].astype(BF16), wk_ref[h])
        roped = mq_ref[:, base + LANE:base + 2 * LANE] * cos_t + mq_ref[:, base + 2 * LANE:base + 3 * LANE] * sin_t
        q_ref[:, h * QPAD:h * QPAD + DK] = q_lat.astype(q_ref.dtype)
        q_ref[:, h * QPAD + DK:(h + 1) * QPAD] = roped.astype(q_ref.dtype)


def _mla_query(mq, wk, cos_t, sin_t, dtype):
    m = mq.shape[0]
    tm = min(256, m)
    return pl.pallas_call(
        _mlaq_kernel, grid=(m // tm,),
        in_specs=[pl.BlockSpec((tm, HEADS * MQ_HEAD), lambda i: (i, 0)),
                  pl.BlockSpec((HEADS, DK, DK), lambda i: (0, 0, 0)),
                  pl.BlockSpec((tm, LANE), lambda i: (i, 0)), pl.BlockSpec((tm, LANE), lambda i: (i, 0))],
        out_specs=pl.BlockSpec((tm, HEADS * QPAD), lambda i: (i, 0)),
        out_shape=jax.ShapeDtypeStruct((m, HEADS * QPAD), dtype),
        compiler_params=_cp(("parallel",)), name="mla_query")(mq, wk, cos_t, sin_t)


def _compress_rows(rows_ref, x_scr, pos_ref, w1_ref, w2_ref, j):
    nb = x_scr.shape[0]
    for c in range(CMP_BLOCK):
        rows = rows_ref[c] if len(rows_ref.shape) == 3 else rows_ref[pl.ds(c, nb, stride=CMP_BLOCK), :]
        x_scr[:, c * DK:(c + 1) * DK] = (rows + pos_ref[j, c:c + 1, :]).astype(BF16)
    hid = _dot(x_scr[...], w1_ref[j])
    return _dot(_silu(hid).astype(BF16), w2_ref[j])


def _cmp_prompt_kernel(rows_ref, pos_ref, w1_ref, w2_ref, o_ref, x_scr):
    o_ref[0] = _compress_rows(rows_ref, x_scr, pos_ref, w1_ref, w2_ref, 0).astype(o_ref.dtype)


def _compress_prompt(p, lay, pos, w1, w2):
    t = p.shape[0]
    nb = t // CMP_BLOCK
    hid = w1.shape[-1]
    base = lay["nsa_kv"] // DK
    return pl.pallas_call(
        _cmp_prompt_kernel, grid=(2,),
        in_specs=[pl.BlockSpec((t, DK), lambda j: (0, base + j)),
                  pl.BlockSpec((1, CMP_BLOCK, DK), lambda j: (j, 0, 0)),
                  pl.BlockSpec((1, CMP_BLOCK * DK, hid), lambda j: (j, 0, 0)),
                  pl.BlockSpec((1, hid, DK), lambda j: (j, 0, 0))],
        out_specs=pl.BlockSpec((1, nb, DK), lambda j: (j, 0, 0)),
        out_shape=jax.ShapeDtypeStruct((2, nb, DK), BF16),
        scratch_shapes=[pltpu.VMEM((nb, CMP_BLOCK * DK), BF16)],
        compiler_params=_cp(("parallel",)), name="compress_prompt")(p, pos, w1, w2)


def _cmp_sample_kernel(pt_ref, pos_ref, w1_ref, w2_ref, cache_ref, kc_ref, vc_ref, buf, sem, x_scr, *, layer, pps):
    s = pl.program_id(0)
    n = pl.num_programs(0)
    slot = s % 2
    per_page = cache_ref.shape[2] // CMP_BLOCK
    fetch = lambda step, sl: [pltpu.make_async_copy(
        cache_ref.at[layer, pt_ref[step * pps + pg], pl.ds(blk * CMP_BLOCK, CMP_BLOCK), j, :],
        buf.at[sl, j, :, pg * per_page + blk, :], sem.at[sl])
        for j in range(2) for pg in range(pps) for blk in range(per_page)]

    @pl.when(s == 0)
    def _():
        for cp in fetch(0, 0):
            cp.start()

    @pl.when(s + 1 < n)
    def _():
        for cp in fetch(s + 1, 1 - slot):
            cp.start()

    for cp in fetch(s, slot):
        cp.wait()
    kc_ref[...] = _compress_rows(buf.at[slot, 0], x_scr, pos_ref, w1_ref, w2_ref, 0).astype(kc_ref.dtype)
    vc_ref[...] = _compress_rows(buf.at[slot, 1], x_scr, pos_ref, w1_ref, w2_ref, 1).astype(vc_ref.dtype)


def _compress_sample(pt_flat, cache_nsa, layer, pos, w1, w2, pps):
    page = cache_nsa.shape[2]
    n_tot = pt_flat.shape[0]
    nb = pps * page // CMP_BLOCK
    hid = w1.shape[-1]
    full = lambda shape: pl.BlockSpec(shape, lambda s, pt: (0,) * len(shape))
    grid_spec = pltpu.PrefetchScalarGridSpec(
        num_scalar_prefetch=1, grid=(n_tot // pps,),
        in_specs=[full((2, CMP_BLOCK, DK)), full((2, CMP_BLOCK * DK, hid)), full((2, hid, DK)),
                  pl.BlockSpec(memory_space=pl.ANY)],
        out_specs=[pl.BlockSpec((nb, DK), lambda s, pt: (s, 0)), pl.BlockSpec((nb, DK), lambda s, pt: (s, 0))],
        scratch_shapes=[pltpu.VMEM((2, 2, CMP_BLOCK, nb, DK), F32), pltpu.SemaphoreType.DMA((2,)),
                        pltpu.VMEM((nb, CMP_BLOCK * DK), BF16)])
    n_blocks = n_tot * page // CMP_BLOCK
    return pl.pallas_call(
        functools.partial(_cmp_sample_kernel, layer=layer, pps=pps), grid_spec=grid_spec,
        out_shape=[jax.ShapeDtypeStruct((n_blocks, DK), BF16), jax.ShapeDtypeStruct((n_blocks, DK), BF16)],
        compiler_params=_cp(("arbitrary",)), name="compress_sample")(pt_flat, pos, w1, w2, cache_nsa)


def _select_blocks(score, blk):
    sel = jnp.zeros(score.shape, F32)
    picks = []
    for _ in range(N_SELECT):
        top = jnp.max(score, axis=-1, keepdims=True)
        pick = jnp.min(jnp.where(score == top, blk, jnp.int32(1 << 30)), axis=-1, keepdims=True)
        hit = blk == pick
        sel = jnp.where(hit & (top > NEG / 2), 1.0, sel)
        score = jnp.where(hit, -3e38, score)
        picks.append(pick)
    return sel, picks


def _block_scores(imp, blk, cur):
    forced = (blk == 0) | (blk == cur) | (blk == cur - 1)
    return jnp.where(blk > cur, NEG, jnp.where(forced, FORCE_SCORE, imp))


def _cmp_attn_prompt_kernel(q_ref, kc_ref, vc_ref, near_ref, far_ref, o_ref, sel_ref):
    qt = pl.program_id(0)
    nbp = kc_ref.shape[0]
    q = jnp.concatenate([q_ref[:, h * DK:(h + 1) * DK].astype(BF16) for h in range(HEADS)], axis=0)
    s = _dot_t(q, kc_ref[...]) * NSA_SCALE
    lane = lax.broadcasted_iota(I32, (ROWS, nbp), 1)
    qpos = qt * TQ + (lax.broadcasted_iota(I32, (ROWS, nbp), 0) & (TQ - 1))
    bias = jnp.broadcast_to(far_ref[:, 0:1], (ROWS, nbp))
    for i in range(4):
        bias = jnp.where(lane == 2 * qt + (i - 2), near_ref[:, i:i + 1], bias)
    valid = CMP_BLOCK * lane + (CMP_BLOCK - 1) <= qpos
    s = jnp.where(valid, s + bias, NEG)
    e = jnp.where(valid, jnp.exp(s - jnp.max(s, axis=-1, keepdims=True)), 0.0)
    den = jnp.sum(e, axis=-1, keepdims=True)
    p = e / jnp.where(den > 0.0, den, 1.0)
    o = _dot(p.astype(BF16), vc_ref[...])
    imp = p[0:TQ]
    for h in range(1, HEADS):
        imp = imp + p[h * TQ:(h + 1) * TQ]
    for h in range(HEADS):
        o_ref[:, h * DK:(h + 1) * DK] = o[h * TQ:(h + 1) * TQ]
    blk = lax.broadcasted_iota(I32, (TQ, nbp), 1)
    cur = (qt * TQ + lax.broadcasted_iota(I32, (TQ, nbp), 0)) // CMP_BLOCK
    sel, _ = _select_blocks(_block_scores(imp, blk, cur), blk)
    sel_ref[...] = sel


def _cmp_attn_prompt(p, lay, kc, vc, near, far):
    t = p.shape[0]
    nbp = kc.shape[0]
    full = lambda a: pl.BlockSpec(a.shape, lambda i: (0,) * a.ndim)
    return pl.pallas_call(
        _cmp_attn_prompt_kernel, grid=(t // TQ,),
        in_specs=[pl.BlockSpec((TQ, NSA_W), functools.partial(lambda i, o: (i, o), o=lay["nsa_q"] // NSA_W)),
                  full(kc), full(vc), full(near), full(far)],
        out_specs=[pl.BlockSpec((TQ, NSA_W), lambda i: (i, 0)), pl.BlockSpec((TQ, nbp), lambda i: (i, 0))],
        out_shape=[jax.ShapeDtypeStruct((t, NSA_W), F32), jax.ShapeDtypeStruct((t, nbp), F32)],
        compiler_params=_cp(("parallel",)), name="cmp_attn_prompt")(p, kc, vc, near, far)


def _attn_prompt_kernel(*refs, dq, scale, mode):
    if mode == "sel":
        q_ref, k_ref, vt_ref, tab_ref, sel_ref, o_ref, q_scr, m_scr, l_scr, acc_scr = refs
    else:
        q_ref, k_ref, vt_ref, tab_ref, o_ref, q_scr, m_scr, l_scr, acc_scr = refs
    qt = pl.program_id(0)
    for h in range(HEADS):
        qh = q_ref[:, h * dq:(h + 1) * dq].astype(BF16)
        if mode == "sel":
            qh = jnp.concatenate([qh, ((sel_ref[...] - 1.0) * -NEG).astype(BF16)], axis=1)
        q_scr[h * TQ:(h + 1) * TQ, :] = qh
    m_scr[...] = jnp.full(m_scr.shape, NEG, F32)
    l_scr[...] = jnp.zeros(l_scr.shape, F32)
    acc_scr[...] = jnp.zeros(acc_scr.shape, F32)
    scale2 = scale * LOG2E

    def logits_of(kt):
        row = pl.multiple_of(kt * TQ, TQ)
        k = k_ref[pl.ds(row, TQ), :]
        if mode == "sel":
            nbp = sel_ref.shape[1]
            key = lax.broadcasted_iota(I32, (TQ, nbp), 0)
            blk = lax.broadcasted_iota(I32, (TQ, nbp), 1)
            expand = jnp.where(blk == 2 * kt + key // CMP_BLOCK, 1.0, 0.0).astype(BF16)
            k = jnp.concatenate([k, expand], axis=1)
        return _dot_t(k, q_scr[...])

    def softmax_update(logits, tab):
        weights, alphas = [], []
        for h in range(HEADS):
            s = logits[:, h * TQ:(h + 1) * TQ] * scale2
            if tab is not None:
                s = s + tab_ref[tab, h]
            m_prev = m_scr[h:h + 1, :]
            m_new = jnp.maximum(m_prev, jnp.max(s, axis=0, keepdims=True))
            alpha = jnp.exp2(m_prev - m_new)
            e = jnp.exp2(s - m_new)
            l_scr[h:h + 1, :] = alpha * l_scr[h:h + 1, :] + jnp.sum(e, axis=0, keepdims=True)
            m_scr[h:h + 1, :] = m_new
            weights.append(e.astype(BF16))
            alphas.append(alpha)
        return jnp.concatenate(weights, axis=1), jnp.concatenate(alphas, axis=1)

    def accumulate(kt, weights, alpha):
        acc_scr[...] = alpha * acc_scr[...] + _dot(vt_ref[kt], weights)

    def group(tiles):
        logits = [logits_of(kt) for kt in tiles]
        parts = [softmax_update(lg, None) for lg in logits]
        pvs = [_dot(vt_ref[kt], w) for kt, (w, _) in zip(tiles, parts)]
        acc = acc_scr[...]
        for (_, a), pv in zip(parts, pvs):
            acc = a * acc + pv
        acc_scr[...] = acc

    def stream(lo, hi):
        n_groups = (hi - lo) // UNROLL

        def grouped(i, carry):
            group([lo + i * UNROLL + u for u in range(UNROLL)])
            return carry

        def single(kt, carry):
            group([kt])
            return carry

        lax.fori_loop(0, n_groups, grouped, 0)
        lax.fori_loop(lo + n_groups * UNROLL, hi, single, 0)

    def special(delta, tab):
        @pl.when(qt >= delta)
        def _():
            w, a = softmax_update(logits_of(qt - delta), tab)
            accumulate(qt - delta, w, a)

    n_win = WINDOW // TQ
    if mode == "mla":
        stream(0, qt)
        special(0, 0)
    elif mode == "sel":
        stream(0, jnp.maximum(qt - 1, 0))
        special(1, 0)
        special(0, 1)
    else:
        special(n_win, 0)
        stream(jnp.maximum(qt - n_win + 1, 0), jnp.maximum(qt - 1, 0))
        special(1, 1)
        special(0, 2)
    for h in range(HEADS):
        o_ref[:, h * DK:(h + 1) * DK] = (acc_scr[:, h * TQ:(h + 1) * TQ] / l_scr[h:h + 1, :]).T


def _key_major(v):
    return jnp.transpose(v.reshape(v.shape[0] // TQ, TQ, DK), (0, 2, 1))


def _attn_prompt(q, q_col, dq, k, k_col, vt, tab, scale, mode, sel=None):
    t = k.shape[0]
    full = lambda a: pl.BlockSpec(a.shape, lambda i: (0,) * a.ndim)
    in_specs = [pl.BlockSpec((TQ, HEADS * dq), functools.partial(lambda i, o: (i, o), o=q_col)),
                pl.BlockSpec((t, dq), functools.partial(lambda i, o: (0, o), o=k_col)),
                full(vt), full(tab)]
    args = [q, k, vt, tab]
    dqa = dq
    if mode == "sel":
        in_specs.append(pl.BlockSpec((TQ, sel.shape[1]), lambda i: (i, 0)))
        args.append(sel)
        dqa = dq + sel.shape[1]
    return pl.pallas_call(
        functools.partial(_attn_prompt_kernel, dq=dq, scale=scale, mode=mode), grid=(t // TQ,),
        in_specs=in_specs,
        out_specs=pl.BlockSpec((TQ, NSA_W), lambda i: (i, 0)),
        out_shape=jax.ShapeDtypeStruct((t, NSA_W), F32),
        scratch_shapes=[pltpu.VMEM((ROWS, dqa), BF16), pltpu.VMEM((HEADS, TQ), F32),
                        pltpu.VMEM((HEADS, TQ), F32), pltpu.VMEM((DK, ROWS), F32)],
        compiler_params=_cp(("parallel",)), name="attn_prompt_" + mode)(*args)


def _softmax_rows(s):
    e = jnp.exp(s - jnp.max(s, axis=-1, keepdims=True))
    return e / jnp.sum(e, axis=-1, keepdims=True)


def _cmp_attn_sample_kernel(q_ref, kc_ref, vc_ref, tab_ref, o_ref, idx_ref, imp_scr, *, nbs):
    gb, nq, width = imp_scr.shape
    nbp = kc_ref.shape[0] // gb
    imp_scr[...] = jnp.zeros(imp_scr.shape, F32)
    for g in range(gb):
        q = q_ref[g * nq:(g + 1) * nq].reshape(nq * HEADS, DK).astype(BF16)
        kc = kc_ref[g * nbp:(g + 1) * nbp, :]
        p = _softmax_rows(_dot_t(q, kc) * NSA_SCALE + tab_ref[...])
        o_ref[g * nq:(g + 1) * nq] = _dot(p.astype(BF16), vc_ref[g * nbp:(g + 1) * nbp, :]).reshape(nq, HEADS, DK)
        imp_scr[g, :, 0:nbp] = jnp.sum(p.reshape(nq, HEADS, nbp), axis=1)
    blk = lax.broadcasted_iota(I32, imp_scr.shape, 2)
    imp = jnp.where(blk < nbs, imp_scr[...], 0.0)
    _, picks = _select_blocks(_block_scores(imp, blk, nbs), blk)
    lane = lax.broadcasted_iota(I32, idx_ref.shape, 2)
    out = jnp.zeros(idx_ref.shape, I32)
    for i, pick in enumerate(picks):
        out = jnp.where(lane == i, pick, out)
    idx_ref[...] = out


def _cmp_attn_sample(q3, kc, vc, tab, nq, nbs):
    ms = q3.shape[0]
    b = ms // nq
    nbp = kc.shape[0] // b
    gb = math.gcd(b, 8)
    return pl.pallas_call(
        functools.partial(_cmp_attn_sample_kernel, nbs=nbs), grid=(b // gb,),
        in_specs=[pl.BlockSpec((gb * nq, HEADS, DK), lambda i: (i, 0, 0)),
                  pl.BlockSpec((gb * nbp, DK), lambda i: (i, 0)), pl.BlockSpec((gb * nbp, DK), lambda i: (i, 0)),
                  pl.BlockSpec(tab.shape, lambda i: (0, 0))],
        out_specs=[pl.BlockSpec((gb * nq, HEADS, DK), lambda i: (i, 0, 0)),
                   pl.BlockSpec((gb, nq, LANE), lambda i: (i, 0, 0))],
        out_shape=[jax.ShapeDtypeStruct((ms, HEADS, DK), F32), jax.ShapeDtypeStruct((b, nq, LANE), I32)],
        scratch_shapes=[pltpu.VMEM((gb, nq, _align(nbs + 1, LANE)), F32)],
        compiler_params=_cp(("parallel",)), name="cmp_attn_sample")(q3, kc, vc, tab)


def _pad_new_rows(pad_scr, new):
    pad_scr[...] = jnp.zeros(pad_scr.shape, F32)
    pad_scr[0:new.shape[0], :] = new
    return pad_scr[...]


def _win_sample_kernel(q_ref, st_ref, new_ref, tab_ref, o_ref, pad_scr):
    nq = q_ref.shape[0]
    q = q_ref[...].reshape(nq * HEADS, DK).astype(BF16)
    st = st_ref.at[0]
    k_st = st[pl.ds(0, WINDOW, stride=2), :].astype(BF16)
    v_st = st[pl.ds(1, WINDOW, stride=2), :].astype(BF16)
    newp = _pad_new_rows(pad_scr, new_ref[0])
    s = jnp.concatenate([_dot_t(q, k_st), _dot_t(q, newp[:, :DK].astype(BF16))], axis=1)
    p = _softmax_rows(s * NSA_SCALE + tab_ref[...]).astype(BF16)
    o = _dot(p[:, :WINDOW], v_st) + _dot(p[:, WINDOW:], newp[:, DK:].astype(BF16))
    o_ref[...] = o.reshape(nq, HEADS, DK)


def _win_sample(q3, state, new, tab, nq):
    ms = q3.shape[0]
    return pl.pallas_call(
        _win_sample_kernel, grid=(ms // nq,),
        in_specs=[pl.BlockSpec((nq, HEADS, DK), lambda i: (i, 0, 0)),
                  pl.BlockSpec((1, 2 * WINDOW, DK), lambda i: (i, 0, 0)),
                  pl.BlockSpec((1, nq, 2 * DK), lambda i: (i, 0, 0)),
                  pl.BlockSpec(tab.shape, lambda i: (0, 0))],
        out_specs=pl.BlockSpec((nq, HEADS, DK), lambda i: (i, 0, 0)),
        out_shape=jax.ShapeDtypeStruct((ms, HEADS, DK), F32),
        scratch_shapes=[pltpu.VMEM((LANE, 2 * DK), F32)],
        compiler_params=_cp(("parallel",)), name="win_sample")(q3, state, new, tab)


def _sel_copies(cache_ref, layer, idx_ref, pt_ref, step, nq, n_pages, nbs, buf, slot, sem):
    per_page = cache_ref.shape[2] // CMP_BLOCK
    copies = []
    for i in range(nq * N_SELECT):
        j = jnp.minimum(idx_ref[step * nq * N_SELECT + i], nbs - 1)
        page = pt_ref[step * n_pages + j // per_page]
        r0 = pl.multiple_of((j % per_page) * CMP_BLOCK, CMP_BLOCK)
        for plane in range(2):
            src = cache_ref.at[layer, page, pl.ds(r0, CMP_BLOCK), 2 + plane, :]
            dst = buf.at[slot, plane, pl.ds(i * CMP_BLOCK, CMP_BLOCK), :]
            copies.append(pltpu.make_async_copy(src, dst, sem.at[slot]))
    return copies


def _sel_sample_kernel(idx_ref, pt_ref, q_ref, new_ref, tab_ref, cache_ref, o_ref, buf, sem, pad_scr,
                       *, layer, n_pages, nbs):
    s = pl.program_id(0)
    n = pl.num_programs(0)
    nq = q_ref.shape[0]
    slot = s % 2
    fetch = lambda step, sl: _sel_copies(cache_ref, layer, idx_ref, pt_ref, step, nq, n_pages, nbs, buf, sl, sem)

    @pl.when(s == 0)
    def _():
        for cp in fetch(0, 0):
            cp.start()

    @pl.when(s + 1 < n)
    def _():
        for cp in fetch(s + 1, 1 - slot):
            cp.start()

    for cp in fetch(s, slot):
        cp.wait()
    newp = _pad_new_rows(pad_scr, new_ref[0])
    kn, vn = newp[:, :DK].astype(BF16), newp[:, DK:].astype(BF16)
    lane = lax.broadcasted_iota(I32, (HEADS, LANE), 1)
    span = N_SELECT * CMP_BLOCK
    for qi in range(nq):
        q = q_ref[qi].astype(BF16)
        k = buf[slot, 0, qi * span:(qi + 1) * span, :].astype(BF16)
        v = buf[slot, 1, qi * span:(qi + 1) * span, :].astype(BF16)

        def block_bias(i):
            j = idx_ref[(s * nq + qi) * N_SELECT + i]
            near = jnp.where(j == nbs - 1, tab_ref[0, qi], jnp.where(j == nbs - 2, tab_ref[1, qi], tab_ref[3, qi]))
            return jnp.where(j >= nbs, NEG, near)

        bias = jnp.concatenate([jnp.where(lane < CMP_BLOCK, block_bias(2 * g), block_bias(2 * g + 1))
                                for g in range(N_SELECT // 2)], axis=1)
        sc = jnp.concatenate([_dot_t(q, k) * NSA_SCALE + bias, _dot_t(q, kn) * NSA_SCALE + tab_ref[2, qi]], axis=1)
        p = _softmax_rows(sc).astype(BF16)
        o_ref[qi] = _dot(p[:, :span], v) + _dot(p[:, span:], vn)


def _sel_sample(idx_flat, pt_flat, q3, new, tab, cache_nsa, layer, nq, n_pages, nbs):
    ms = q3.shape[0]
    grid_spec = pltpu.PrefetchScalarGridSpec(
        num_scalar_prefetch=2, grid=(ms // nq,),
        in_specs=[pl.BlockSpec((nq, HEADS, DK), lambda i, a, b: (i, 0, 0)),
                  pl.BlockSpec((1, nq, 2 * DK), lambda i, a, b: (i, 0, 0)),
                  pl.BlockSpec(tab.shape, lambda i, a, b: (0, 0, 0, 0)),
                  pl.BlockSpec(memory_space=pl.ANY)],
        out_specs=pl.BlockSpec((nq, HEADS, DK), lambda i, a, b: (i, 0, 0)),
        scratch_shapes=[pltpu.VMEM((2, 2, nq * N_SELECT * CMP_BLOCK, DK), F32), pltpu.SemaphoreType.DMA((2,)),
                        pltpu.VMEM((LANE, 2 * DK), F32)])
    return pl.pallas_call(
        functools.partial(_sel_sample_kernel, layer=layer, n_pages=n_pages, nbs=nbs), grid_spec=grid_spec,
        out_shape=jax.ShapeDtypeStruct((ms, HEADS, DK), F32),
        compiler_params=_cp(("arbitrary",)), name="sel_sample")(idx_flat, pt_flat, q3, new, tab, cache_nsa)


def _mla_sample_kernel(pt_ref, q_ref, new_ref, tab_ref, cache_ref, o_ref, buf, sem, pad_scr, m_scr, l_scr, acc_scr,
                       *, layer, pps, n_chunk, group):
    s = pl.program_id(0)
    n = pl.num_programs(0)
    nq = q_ref.shape[0]
    slot = s % 2
    ch = s % n_chunk
    width = cache_ref.shape[2]
    fetch = lambda step, sl: [pltpu.make_async_copy(cache_ref.at[layer, pt_ref[step * pps + pg]], buf.at[sl, pg],
                                                    sem.at[sl]) for pg in range(pps)]

    @pl.when(s == 0)
    def _():
        for cp in fetch(0, 0):
            cp.start()

    @pl.when(s + 1 < n)
    def _():
        for cp in fetch(s + 1, 1 - slot):
            cp.start()

    for cp in fetch(s, slot):
        cp.wait()

    @pl.when(ch == 0)
    def _():
        m_scr[...] = jnp.full(m_scr.shape, NEG, F32)
        l_scr[...] = jnp.zeros(l_scr.shape, F32)
        acc_scr[...] = jnp.zeros(acc_scr.shape, F32)

    q = q_ref[...].reshape(nq * HEADS, QPAD).astype(BF16)
    q_feat = q[:, :width]
    scale2 = MLA_SCALE * LOG2E

    def update(sc, weighted_values):
        m_prev = m_scr[...]
        m_new = jnp.maximum(m_prev, jnp.max(sc, axis=-1, keepdims=True))
        alpha = jnp.exp2(m_prev - m_new)
        e = jnp.exp2(sc - m_new)
        l_scr[...] = alpha * l_scr[...] + jnp.sum(e, axis=-1, keepdims=True)
        acc_scr[...] = alpha * acc_scr[...] + weighted_values(e.astype(BF16))
        m_scr[...] = m_new

    def body(g, carry):
        pages = [buf[slot, g * group + i].astype(BF16) for i in range(group)]
        sc = jnp.concatenate([_dot(q_feat, pg) for pg in pages], axis=1) * scale2
        page_len = pages[0].shape[1]

        def weighted_values(e):
            out = _dot_t(e[:, 0:page_len], pages[0][:DK, :])
            for i in range(1, group):
                out = out + _dot_t(e[:, i * page_len:(i + 1) * page_len], pages[i][:DK, :])
            return out

        update(sc, weighted_values)
        return carry

    lax.fori_loop(0, pps // group, body, 0)

    @pl.when(ch == n_chunk - 1)
    def _():
        newp = _pad_new_rows(pad_scr, new_ref[0]).astype(BF16)
        update(_dot_t(q, newp) * scale2 + tab_ref[...], lambda e: _dot(e, newp[:, :DK]))
        o_ref[...] = (acc_scr[...] / l_scr[...]).reshape(nq, HEADS, DK)


def _mla_sample(pt_flat, q3, new, tab, cache_t, layer, nq, n_pages, pps):
    ms = q3.shape[0]
    b = ms // nq
    width, page = cache_t.shape[2], cache_t.shape[3]
    n_chunk = n_pages // pps
    group = pps
    grid_spec = pltpu.PrefetchScalarGridSpec(
        num_scalar_prefetch=1, grid=(b * n_chunk,),
        in_specs=[pl.BlockSpec((nq, HEADS, QPAD), lambda s, pt: (s // n_chunk, 0, 0)),
                  pl.BlockSpec((1, nq, QPAD), lambda s, pt: (s // n_chunk, 0, 0)),
                  pl.BlockSpec(tab.shape, lambda s, pt: (0, 0)),
                  pl.BlockSpec(memory_space=pl.ANY)],
        out_specs=pl.BlockSpec((nq, HEADS, DK), lambda s, pt: (s // n_chunk, 0, 0)),
        scratch_shapes=[pltpu.VMEM((2, pps, width, page), F32), pltpu.SemaphoreType.DMA((2,)),
                        pltpu.VMEM((LANE, QPAD), F32), pltpu.VMEM((nq * HEADS, 1), F32),
                        pltpu.VMEM((nq * HEADS, 1), F32), pltpu.VMEM((nq * HEADS, DK), F32)])
    return pl.pallas_call(
        functools.partial(_mla_sample_kernel, layer=layer, pps=pps, n_chunk=n_chunk, group=group),
        grid_spec=grid_spec, out_shape=jax.ShapeDtypeStruct((ms, HEADS, DK), F32),
        compiler_params=_cp(("arbitrary",)), name="mla_sample")(pt_flat, q3, new, tab, cache_t)


def _post_kernel(oc_ref, os_ref, ow_ref, g1_ref, nz_ref, mz_ref, lat_ref, wv_ref, v0_ref, v1_ref, v2_ref,
                 cb_ref, cz_ref, cw_ref, brc_ref, brn_ref, brm_ref):
    g = _sigmoid(g1_ref[...])
    for h in range(HEADS):
        hs = slice(h * DK, (h + 1) * DK)
        gate = lambda br: g[:, ROPE_DIM + br * HEADS + h:ROPE_DIM + br * HEADS + h + 1]
        o = gate(0) * oc_ref[:, hs] + gate(1) * os_ref[:, hs] + gate(2) * ow_ref[:, hs]
        brn_ref[:, hs] = (o * _silu(nz_ref[:, hs])).astype(BF16)
        mo = _dot(lat_ref[:, hs].astype(BF16), wv_ref[h])
        brm_ref[:, hs] = (mo * _silu(mz_ref[:, hs])).astype(BF16)
    y = cw_ref[0:1, :] * v0_ref[...] + cw_ref[1:2, :] * v1_ref[...] + cw_ref[2:3, :] * v2_ref[...]
    brc_ref[...] = (cb_ref[...] * y * _silu(cz_ref[...])).astype(BF16)


def _post_mix(o_c, o_s, o_w, p, lay, lat, wv, v0, v1, v2, conv_w, c):
    m = p.shape[0]
    tm = min(256, m)
    col = lambda name, w: pl.BlockSpec((tm, w), functools.partial(lambda i, o: (i, o), o=lay[name] // w))
    row = lambda w: pl.BlockSpec((tm, w), lambda i: (i, 0))
    return pl.pallas_call(
        _post_kernel, grid=(m // tm,),
        in_specs=[row(NSA_W), row(NSA_W), row(NSA_W), col("g1", LANE), col("nsa_z", NSA_W), col("mla_z", NSA_W),
                  row(NSA_W), pl.BlockSpec((HEADS, DK, DK), lambda i: (0, 0, 0)), row(c), row(c), row(c),
                  col("conv_b", c), col("conv_z", c), pl.BlockSpec(conv_w.shape, lambda i: (0, 0))],
        out_specs=[row(c), row(NSA_W), row(NSA_W)],
        out_shape=[jax.ShapeDtypeStruct((m, c), BF16), jax.ShapeDtypeStruct((m, NSA_W), BF16),
                   jax.ShapeDtypeStruct((m, NSA_W), BF16)],
        compiler_params=_cp(("parallel",)), name="post_mix",
    )(o_c, o_s, o_w, p, p, p, lat, wv, v0, v1, v2, p, p, conv_w)


def _merge_kernel(brc_ref, brn_ref, brm_ref, wc_ref, wn_ref, wm_ref, g0_ref, g1_ref, g2_ref, o_ref):
    merged = (_sigmoid(g0_ref[...]) * _dot(brc_ref[...], wc_ref[...])
              + _sigmoid(g1_ref[...]) * _dot(brn_ref[...], wn_ref[...])
              + _sigmoid(g2_ref[...]) * _dot(brm_ref[...], wm_ref[...]))
    o_ref[...] = merged.astype(BF16)


def _merge(brc, brn, brm, wc, wn, wm, p, lay, d):
    m = p.shape[0]
    tm = min(512, m)
    tn = _col_tile(d, 512)
    act = lambda a: pl.BlockSpec((tm, a.shape[1]), lambda j, i: (i, 0))
    wgt = lambda a: pl.BlockSpec((a.shape[0], tn), lambda j, i: (0, j))
    gate = lambda br: pl.BlockSpec((tm, tn), functools.partial(lambda j, i, o: (i, o + j), o=(lay["merge_g"] + br * d) // tn))
    return pl.pallas_call(
        _merge_kernel, grid=(d // tn, m // tm),
        in_specs=[act(brc), act(brn), act(brm), wgt(wc), wgt(wn), wgt(wm), gate(0), gate(1), gate(2)],
        out_specs=pl.BlockSpec((tm, tn), lambda j, i: (i, j)),
        out_shape=jax.ShapeDtypeStruct((m, d), BF16),
        compiler_params=_cp(("parallel", "parallel")), name="merge")(brc, brn, brm, wc, wn, wm, p, p, p)


def _project(x, lw):
    h = _rmsnorm(x, lw["norm_g"], BF16)
    p = _matmul(h, lw["w_in"], name="in_proj")
    conv_v, cqn, mrow, kpad, kvb = _pre_mix(p, lw["lay"], lw["c"], lw["ql"], lw["g_q"], lw["g_kv"], lw["cos"], lw["sin"])
    mq = _matmul(cqn, lw["w_uq"], tn_cap=1024, name="q_up_proj")
    return p, conv_v, mrow, kpad, kvb, mq


def _finish(x, p, lw, o_c, o_s, o_w, lat, v0, v1, v2):
    brc, brn, brm = _post_mix(o_c, o_s, o_w, p, lw["lay"], lat, lw["wv"], v0, v1, v2, lw["conv_w"], lw["c"])
    merged = _merge(brc, brn, brm, lw["wc"], lw["wn"], lw["wm"], p, lw["lay"], x.shape[1])
    return _matmul(merged, lw["w_out"], res=x, tn_cap=1024, name="out_proj")


def _prompt_layer(x, lw, tabs):
    lay, c = lw["lay"], lw["c"]
    t = x.shape[0]
    p, conv_v, mrow, kpad, kvb, mq = _project(x, lw)
    q_mla = _mla_query(mq, lw["wk"], lw["cos"], lw["sin"], BF16)
    zc = lambda n: jnp.zeros((n, c), F32)
    v1 = jnp.concatenate([zc(1), conv_v[:-1]], axis=0)
    v0 = jnp.concatenate([zc(2), conv_v[:-2]], axis=0)
    sel_tab, win_tab, mla_tab, cmp_near, far = tabs
    kcv = _compress_prompt(p, lay, lw["phi_pos"], lw["phi_w1"], lw["phi_w2"])
    nbp = _align(kcv.shape[1], LANE)
    kcv = jnp.pad(kcv, ((0, 0), (0, nbp - kcv.shape[1]), (0, 0)))
    o_c, sel = _cmp_attn_prompt(p, lay, kcv[0], kcv[1], cmp_near, far)
    q_col = lay["nsa_q"] // NSA_W
    o_s = _attn_prompt(p, q_col, DK, kvb, 2, _key_major(kvb[:, 3 * DK:4 * DK]), sel_tab, NSA_SCALE, "sel", sel)
    o_w = _attn_prompt(p, q_col, DK, kvb, 4, _key_major(kvb[:, 5 * DK:6 * DK]), win_tab, NSA_SCALE, "win")
    lat = _attn_prompt(q_mla, 0, QPAD, kpad, 0, _key_major(kpad[:, :DK]), mla_tab, MLA_SCALE, "mla")
    x_new = _finish(x, p, lw, o_c, o_s, o_w, lat, v0, v1, conv_v)
    kv0 = lay["nsa_kv"]
    wb = min(WINDOW, t)
    new_nsa = p[:, kv0:kv0 + 4 * DK].reshape(1, t, 4, DK)
    new_win = p[t - wb:, kv0 + 4 * DK:kv0 + 6 * DK].reshape(1, wb, 2, DK)
    return x_new, mrow[None], new_nsa, new_win, conv_v[None, -2:]


def _sample_layer(x, lw, tabs, layer, cache_mla_t, cache_nsa, state_win, state_conv, pt_flat, b, nq, n_pages):
    lay, c = lw["lay"], lw["c"]
    ms = x.shape[0]
    page = cache_nsa.shape[2]
    past = n_pages * page
    nbs = past // CMP_BLOCK
    cmp_tab, win_tab, sel_tab, mla_new = tabs
    p, conv_v, mrow, kpad, kvb, mq = _project(x, lw)
    q_mla = _mla_query(mq, lw["wk"], lw["cos"], lw["sin"], F32).reshape(ms, HEADS, QPAD)
    vp = jnp.concatenate([state_conv, conv_v.reshape(b, nq, c)], axis=1)
    v0, v1, v2 = (vp[:, i:i + nq].reshape(ms, c) for i in range(3))
    q3 = p[:, lay["nsa_q"]:lay["nsa_q"] + NSA_W].reshape(ms, HEADS, DK)
    kv0 = lay["nsa_kv"]
    kv_new = p[:, kv0:kv0 + KV_COLS].reshape(b, nq, 6, DK)
    kc, vc = _compress_sample(pt_flat, cache_nsa, layer, lw["phi_pos"], lw["phi_w1"], lw["phi_w2"], min(64, n_pages))
    nbp = cmp_tab.shape[1]
    padb = lambda a: jnp.pad(a.reshape(b, nbs, DK), ((0, 0), (0, nbp - nbs), (0, 0))).reshape(b * nbp, DK)
    o_c, idx = _cmp_attn_sample(q3, padb(kc), padb(vc), cmp_tab, nq, nbs)
    idx_flat = idx[:, :, :N_SELECT].reshape(-1)
    o_s = _sel_sample(idx_flat, pt_flat, q3, kv_new[:, :, 2:4].reshape(b, nq, 2 * DK), sel_tab, cache_nsa, layer,
                      nq, n_pages, nbs)
    win_new = kv_new[:, :, 4:6]
    o_w = _win_sample(q3, state_win.reshape(b, 2 * WINDOW, DK), win_new.reshape(b, nq, 2 * DK), win_tab, nq)
    mrow_pad = jnp.pad(mrow, ((0, 0), (0, QPAD - mrow.shape[1]))).reshape(b, nq, QPAD)
    lat = _mla_sample(pt_flat, q_mla, mrow_pad, mla_new, cache_mla_t, layer, nq, n_pages, min(64, n_pages))
    flat = lambda a: a.reshape(ms, NSA_W)
    x_new = _finish(x, p, lw, flat(o_c), flat(o_s), flat(o_w), flat(lat), v0, v1, v2)
    new_win_state = jnp.concatenate([state_win, win_new], axis=1)[:, -WINDOW:]
    return (x_new, mrow.reshape(b, nq, -1), kv_new[:, :, 0:4], new_win_state, vp[:, -2:])


def kernel(x_prompt, x_sample, cache_mla, cache_nsa, state_nsa_win, state_conv, page_table, norm_g, w_in, conv_w,
           phi_pos, phi_w1, phi_w2, mla_q_norm, mla_kv_norm, w_uq, w_ukv, rel_bias, w_branch, w_out, final_g):
    depth = w_in.shape[0]
    bp, t, d = x_prompt.shape
    b, nq, _ = x_sample.shape
    c = conv_w.shape[-1]
    ql = mla_q_norm.shape[-1]
    hid = phi_w2.shape[2]
    n_pages = page_table.shape[1]
    page = cache_mla.shape[2]
    past = n_pages * page
    assert bp == 1 and t % TQ == 0 and t >= WINDOW and WINDOW % TQ == 0
    assert state_nsa_win.shape[2] == WINDOW and past >= WINDOW and past % CMP_BLOCK == 0
    assert (past + nq) // CMP_BLOCK == past // CMP_BLOCK and past // CMP_BLOCK >= N_SELECT
    assert page % CMP_BLOCK == 0 and nq <= 8 and w_ukv.shape[1:] == (DK, HEADS, 2 * DK)

    lay, n_pack = _pack_layout(c, ql, d)
    tbl = rel_bias[jnp.asarray(_bucket_of_dist())].T.astype(F32)
    nbs = past // CMP_BLOCK
    tabs_p = _prompt_tables(tbl)
    tabs_s = _sample_tables(tbl, past, nq, _align(nbs, LANE))
    cos_p, sin_p = _rope_tables(np.arange(t))
    cos_s, sin_s = _rope_tables(np.tile(past + np.arange(nq), b))
    cache_mla_t = jnp.swapaxes(cache_mla, 2, 3)
    pt_flat = page_table.reshape(-1)

    xp = x_prompt.reshape(t, d)
    xs = x_sample.reshape(b * nq, d)
    outs = [[] for _ in range(8)]
    for l in range(depth):
        lw = {
            "lay": lay, "c": c, "ql": ql, "norm_g": norm_g[l],
            "w_in": _pack_w_in(w_in[l], lay, n_pack, c, ql, d),
            "g_q": mla_q_norm[l], "g_kv": mla_kv_norm[l], "w_uq": _pack_w_uq(w_uq[l]),
            "wk": jnp.transpose(w_ukv[l][:, :, :DK], (1, 2, 0)).astype(BF16),
            "wv": jnp.transpose(w_ukv[l][:, :, DK:], (1, 0, 2)).astype(BF16),
            "phi_pos": phi_pos[l], "phi_w1": phi_w1[l].reshape(2, CMP_BLOCK * DK, hid).astype(BF16),
            "phi_w2": phi_w2[l].astype(BF16), "conv_w": conv_w[l],
            "wc": w_branch[l][:c].astype(BF16), "wn": w_branch[l][c:c + NSA_W].astype(BF16),
            "wm": w_branch[l][c + NSA_W:].astype(BF16), "w_out": w_out[l].astype(BF16),
        }
        xp, mla_p, nsa_p, win_p, conv_p = _prompt_layer(xp, dict(lw, cos=cos_p, sin=sin_p), tabs_p)
        xs, mla_s, nsa_s, win_s, conv_s = _sample_layer(
            xs, dict(lw, cos=cos_s, sin=sin_s), tabs_s, l, cache_mla_t, cache_nsa, state_nsa_win[l], state_conv[l],
            pt_flat, b, nq, n_pages)
        for lst, val in zip(outs, (mla_p, mla_s, nsa_p, nsa_s, win_p, win_s, conv_p, conv_s)):
            lst.append(val)
    y_prompt = _rmsnorm(xp, final_g, F32).reshape(1, t, d)
    y_sample = _rmsnorm(xs, final_g, F32).reshape(b, nq, d)
    return (y_prompt, y_sample) + tuple(jnp.stack(o) for o in outs)
```

```python
import functools
import math

import jax
import jax.numpy as jnp
import numpy as np
from jax import lax
from jax.experimental import pallas as pl
from jax.experimental.pallas import tpu as pltpu

F32 = jnp.float32
BF16 = jnp.bfloat16
I32 = jnp.int32

HEADS = 8
DK = 128
ROPE_DIM = 32
ROPE_THETA = 10000.0
CMP_BLOCK = 64
N_SELECT = 16
WINDOW = 512
N_BUCKETS = 32
MAX_DIST = 128
N_BRANCH = 3
RMS_EPS = 1e-6
NEG = -1e30
FORCE_SCORE = 1e4
LOG2E = math.log2(math.e)
NSA_SCALE = DK ** -0.5
MLA_SCALE = (DK + ROPE_DIM) ** -0.5
NSA_W = HEADS * DK
KV_COLS = 6 * DK

LANE = 128
TQ = 128
ROWS = TQ * HEADS
UNROLL = 4
MQ_HEAD = 3 * LANE
QPAD = 2 * LANE
VMEM_LIMIT = 56 * 1024 * 1024


def _cp(sem, vmem=VMEM_LIMIT):
    return pltpu.CompilerParams(dimension_semantics=sem, vmem_limit_bytes=vmem)


def _align(x, a):
    return -(-x // a) * a


def _dot(a, b):
    return jnp.dot(a, b, preferred_element_type=F32)


def _dot_t(a, b):
    return lax.dot_general(a, b, (((1,), (1,)), ((), ())), preferred_element_type=F32)


def _silu(x):
    return x / (1.0 + jnp.exp(-x))


def _sigmoid(x):
    return 1.0 / (1.0 + jnp.exp(-x))


def _rms_kernel(x_ref, g_ref, o_ref):
    x = x_ref[...]
    y = x * lax.rsqrt(jnp.mean(x * x, axis=-1, keepdims=True) + RMS_EPS)
    o_ref[...] = (y * g_ref[...]).astype(o_ref.dtype)


def _rmsnorm(x, g, dtype):
    m, d = x.shape
    tm = min(256, m)
    return pl.pallas_call(
        _rms_kernel, grid=(m // tm,),
        in_specs=[pl.BlockSpec((tm, d), lambda i: (i, 0)), pl.BlockSpec((1, d), lambda i: (0, 0))],
        out_specs=pl.BlockSpec((tm, d), lambda i: (i, 0)),
        out_shape=jax.ShapeDtypeStruct((m, d), dtype),
        compiler_params=_cp(("parallel",)), name="rmsnorm")(x, g.reshape(1, d))


def _mm_kernel(a_ref, b_ref, o_ref):
    o_ref[...] = _dot(a_ref[...], b_ref[...]).astype(o_ref.dtype)


def _mm_res_kernel(a_ref, b_ref, r_ref, o_ref):
    o_ref[...] = r_ref[...] + _dot(a_ref[...], b_ref[...])


def _col_tile(n, cap):
    t = min(cap, n) // LANE * LANE
    while n % t:
        t -= LANE
    return t


def _matmul(a, b, res=None, tn_cap=1280, name="matmul"):
    m, k = a.shape
    n = b.shape[1]
    tm = min(512, m)
    tn = _col_tile(n, tn_cap)
    in_specs = [pl.BlockSpec((tm, k), lambda j, i: (i, 0)), pl.BlockSpec((k, tn), lambda j, i: (0, j))]
    args = [a, b]
    kern = _mm_kernel
    if res is not None:
        in_specs.append(pl.BlockSpec((tm, tn), lambda j, i: (i, j)))
        args.append(res)
        kern = _mm_res_kernel
    return pl.pallas_call(
        kern, grid=(n // tn, m // tm), in_specs=in_specs,
        out_specs=pl.BlockSpec((tm, tn), lambda j, i: (i, j)),
        out_shape=jax.ShapeDtypeStruct((m, n), F32),
        compiler_params=_cp(("parallel", "parallel")), name=name)(*args)


def _pack_layout(c, ql, d):
    groups = [("conv_u", c, c), ("conv_b", c, c), ("conv_c", c, c), ("conv_z", c, c),
              ("nsa_q", NSA_W, NSA_W), ("nsa_z", NSA_W, NSA_W), ("mla_z", NSA_W, NSA_W),
              ("merge_g", N_BRANCH * d, min(d, 1024)), ("mla_cq", ql, ql), ("nsa_kv", KV_COLS, KV_COLS),
              ("mla_ckv", DK, LANE), ("g1", LANE, LANE), ("g2", LANE, LANE)]
    off, lay = 0, {}
    for name, w, a in groups:
        off = _align(off, a)
        lay[name] = off
        off += w
    return lay, _align(off, LANE)


def _pack_w_in(w, lay, n_pack, c, ql, d):
    src, off = {}, 0
    for name, n in (("conv_u", c), ("conv_b", c), ("conv_c", c), ("conv_z", c), ("nsa_q", NSA_W),
                    ("nsa_kv", KV_COLS), ("nsa_g", N_BRANCH * HEADS), ("nsa_z", NSA_W), ("mla_cq", ql),
                    ("mla_ckv", DK), ("mla_kr", ROPE_DIM), ("mla_z", NSA_W), ("merge_g", N_BRANCH * d)):
        src[name] = w[:, off:off + n]
        off += n
    half = ROPE_DIM // 2
    kr = src["mla_kr"]
    kr_sw = jnp.concatenate([kr[:, half:], kr[:, :half]], axis=1)
    zeros = lambda n: jnp.zeros((w.shape[0], n), w.dtype)
    dst = dict(src)
    dst["g1"] = jnp.concatenate([kr, src["nsa_g"], zeros(LANE - ROPE_DIM - N_BRANCH * HEADS)], axis=1)
    dst["g2"] = jnp.concatenate([kr_sw, zeros(LANE - ROPE_DIM)], axis=1)
    pieces, pos = [], 0
    for name, o in sorted(lay.items(), key=lambda kv: kv[1]):
        if o > pos:
            pieces.append(zeros(o - pos))
        pieces.append(dst[name])
        pos = o + dst[name].shape[1]
    if n_pack > pos:
        pieces.append(zeros(n_pack - pos))
    return jnp.concatenate(pieces, axis=1).astype(BF16)


def _pack_w_uq(w_uq):
    ql = w_uq.shape[0]
    w = w_uq.reshape(ql, HEADS, DK + ROPE_DIM)
    half = ROPE_DIM // 2
    nope, r = w[..., :DK], w[..., DK:]
    r_sw = jnp.concatenate([r[..., half:], r[..., :half]], axis=-1)
    z = jnp.zeros((ql, HEADS, LANE - ROPE_DIM), w.dtype)
    return jnp.concatenate([nope, r, z, r_sw, z], axis=-1).reshape(ql, HEADS * MQ_HEAD).astype(BF16)


def _rope_tables(pos):
    inv = ROPE_THETA ** (-jnp.arange(0, ROPE_DIM, 2, dtype=F32) / ROPE_DIM)
    ang = jnp.asarray(pos, F32)[:, None] * inv
    c, s = jnp.cos(ang), jnp.sin(ang)
    z = jnp.zeros((ang.shape[0], LANE - ROPE_DIM), F32)
    return jnp.concatenate([c, c, z], axis=1), jnp.concatenate([-s, s, z], axis=1)


def _bucket_of_dist():
    d = np.arange(MAX_DIST + 1)
    exact = N_BUCKETS // 2
    large = exact + np.floor(np.log(np.maximum(d, 1) / exact) / math.log(MAX_DIST / exact)
                             * (N_BUCKETS - exact) + 1e-9).astype(np.int64)
    return np.where(d < exact, d, np.minimum(large, N_BUCKETS - 1)).astype(np.int32)


def _bias_lookup(tbl, dist, valid=None):
    idx = np.clip(dist, 0, MAX_DIST).astype(np.int32)
    b = tbl[:, idx]
    if valid is not None:
        b = jnp.where(jnp.asarray(valid)[None], b, NEG)
    return b


def _prompt_tables(tbl):
    rk = np.arange(TQ)[:, None]
    rq = np.arange(TQ)[None, :]
    d0, d1 = rq - rk, TQ + rq - rk
    far_h = tbl[:, MAX_DIST][:, None, None]
    rel = lambda dist, valid: jnp.where(jnp.asarray(valid)[None], (_bias_lookup(tbl, dist) - far_h) * LOG2E, NEG)
    t0 = rel(d0, d0 >= 0)
    t1 = rel(d1, d1 >= 0)
    mask = lambda valid: jnp.broadcast_to(jnp.where(jnp.asarray(valid), 0.0, NEG), (HEADS, TQ, TQ)).astype(F32)
    sel_tab = jnp.stack([t1, t0])
    win_tab = jnp.stack([mask(rk > rq), t1, t0])
    mla_tab = mask(d0 >= 0)[None]
    m = np.arange(-2, 2)[None, :]
    dc = np.arange(TQ)[:, None] - CMP_BLOCK * m - (CMP_BLOCK - 1)
    near = _bias_lookup(tbl, dc).reshape(ROWS, 4)
    cmp_near = jnp.concatenate([near, jnp.zeros((ROWS, LANE - 4), F32)], axis=1)
    far = _bias_lookup(tbl, np.full((TQ, TQ), MAX_DIST)).reshape(ROWS, TQ)
    return sel_tab, win_tab, mla_tab, cmp_near, far


def _sample_tables(tbl, past, nq, nbp):
    nbs = past // CMP_BLOCK
    qi = np.arange(nq)
    qpos = past + qi
    rows = lambda b: jnp.moveaxis(b, 0, 1).reshape(nq * HEADS, -1)
    n = np.arange(nbp)
    dc = qpos[:, None] - (CMP_BLOCK * n[None, :] + CMP_BLOCK - 1)
    cmp_tab = rows(_bias_lookup(tbl, dc, (dc >= 0) & (n[None, :] < nbs)))
    i = np.arange(WINDOW + LANE)
    kpos = past - WINDOW + i
    dw = qpos[:, None] - kpos[None, :]
    win_tab = rows(_bias_lookup(tbl, dw, (dw >= 0) & (dw < WINDOW) & (i[None, :] < WINDOW + nq)))
    r = np.arange(LANE) % CMP_BLOCK
    d_last = (qpos[:, None] - (nbs - 1) * CMP_BLOCK) - r[None, :]
    d_prev = d_last + CMP_BLOCK
    j = np.arange(LANE)
    d_new = qi[:, None] - j[None, :]
    sel_tab = jnp.stack([jnp.moveaxis(_bias_lookup(tbl, d_last), 0, 1),
                         jnp.moveaxis(_bias_lookup(tbl, d_prev), 0, 1),
                         jnp.moveaxis(_bias_lookup(tbl, d_new, (d_new >= 0) & (j[None, :] < nq)), 0, 1),
                         jnp.moveaxis(_bias_lookup(tbl, np.full((nq, LANE), MAX_DIST)), 0, 1)])
    new_mask = jnp.where(jnp.asarray((d_new >= 0) & (j[None, :] < nq)), 0.0, NEG)
    mla_new = jnp.repeat(new_mask, HEADS, axis=0)
    return cmp_tab, win_tab, sel_tab, mla_new


def _pre_kernel(u_ref, c_ref, cq_ref, kv_ref, ckv_ref, g1_ref, g2_ref, gq_ref, gkv_ref, cos_ref, sin_ref,
                convv_ref, cqn_ref, mrow_ref, kpad_ref, kvb_ref):
    convv_ref[...] = c_ref[...] * u_ref[...]
    cq = cq_ref[...]
    cqn = cq * lax.rsqrt(jnp.mean(cq * cq, axis=-1, keepdims=True) + RMS_EPS) * gq_ref[...]
    cqn_ref[...] = cqn.astype(BF16)
    ckv = ckv_ref[...]
    lat = ckv * lax.rsqrt(jnp.mean(ckv * ckv, axis=-1, keepdims=True) + RMS_EPS) * gkv_ref[...]
    roped = g1_ref[...] * cos_ref[...] + g2_ref[...] * sin_ref[...]
    mrow_ref[:, :DK] = lat
    mrow_ref[:, DK:] = roped[:, :ROPE_DIM]
    kpad_ref[...] = jnp.concatenate([lat, roped], axis=1).astype(BF16)
    kvb_ref[...] = kv_ref[...].astype(BF16)


def _pre_mix(p, lay, c, ql, gq, gkv, cos_t, sin_t):
    m = p.shape[0]
    tm = min(256, m)
    col = lambda name, w: pl.BlockSpec((tm, w), functools.partial(lambda i, o: (i, o), o=lay[name] // w))
    row = lambda w: pl.BlockSpec((tm, w), lambda i: (i, 0))
    par = lambda w: pl.BlockSpec((1, w), lambda i: (0, 0))
    return pl.pallas_call(
        _pre_kernel, grid=(m // tm,),
        in_specs=[col("conv_u", c), col("conv_c", c), col("mla_cq", ql), col("nsa_kv", KV_COLS),
                  col("mla_ckv", DK), col("g1", LANE), col("g2", LANE), par(ql), par(DK), row(LANE), row(LANE)],
        out_specs=[row(c), row(ql), row(DK + ROPE_DIM), row(QPAD), row(KV_COLS)],
        out_shape=[jax.ShapeDtypeStruct((m, c), F32), jax.ShapeDtypeStruct((m, ql), BF16),
                   jax.ShapeDtypeStruct((m, DK + ROPE_DIM), F32), jax.ShapeDtypeStruct((m, QPAD), BF16),
                   jax.ShapeDtypeStruct((m, KV_COLS), BF16)],
        compiler_params=_cp(("parallel",)), name="pre_mix",
    )(p, p, p, p, p, p, p, gq.reshape(1, ql), gkv.reshape(1, DK), cos_t, sin_t)


def _mlaq_kernel(mq_ref, wk_ref, cos_ref, sin_ref, q_ref):
    cos_t, sin_t = cos_ref[...], sin_ref[...]
    for h in range(HEADS):
        base = h * MQ_HEAD
        q_lat = _dot(mq_ref[:, base:base + DK].astype(BF16), wk_ref[h])
        roped = mq_ref[:, base + LANE:base + 2 * LANE] * cos_t + mq_ref[:, base + 2 * LANE:base + 3 * LANE] * sin_t
        q_ref[:, h * QPAD:h * QPAD + DK] = q_lat.astype(q_ref.dtype)
        q_ref[:, h * QPAD + DK:(h + 1) * QPAD] = roped.astype(q_ref.dtype)


def _mla_query(mq, wk, cos_t, sin_t, dtype):
    m = mq.shape[0]
    tm = min(256, m)
    return pl.pallas_call(
        _mlaq_kernel, grid=(m // tm,),
        in_specs=[pl.BlockSpec((tm, HEADS * MQ_HEAD), lambda i: (i, 0)),
                  pl.BlockSpec((HEADS, DK, DK), lambda i: (0, 0, 0)),
                  pl.BlockSpec((tm, LANE), lambda i: (i, 0)), pl.BlockSpec((tm, LANE), lambda i: (i, 0))],
        out_specs=pl.BlockSpec((tm, HEADS * QPAD), lambda i: (i, 0)),
        out_shape=jax.ShapeDtypeStruct((m, HEADS * QPAD), dtype),
        compiler_params=_cp(("parallel",)), name="mla_query")(mq, wk, cos_t, sin_t)


def _compress_rows(rows_ref, x_scr, pos_ref, w1_ref, w2_ref, j):
    nb = x_scr.shape[0]
    for c in range(CMP_BLOCK):
        rows = rows_ref[c] if len(rows_ref.shape) == 3 else rows_ref[pl.ds(c, nb, stride=CMP_BLOCK), :]
        x_scr[:, c * DK:(c + 1) * DK] = (rows + pos_ref[j, c:c + 1, :]).astype(BF16)
    hid = _dot(x_scr[...], w1_ref[j])
    return _dot(_silu(hid).astype(BF16), w2_ref[j])


def _cmp_prompt_kernel(rows_ref, pos_ref, w1_ref, w2_ref, o_ref, x_scr):
    o_ref[0] = _compress_rows(rows_ref, x_scr, pos_ref, w1_ref, w2_ref, 0).astype(o_ref.dtype)


def _compress_prompt(p, lay, pos, w1, w2):
    t = p.shape[0]
    nb = t // CMP_BLOCK
    hid = w1.shape[-1]
    base = lay["nsa_kv"] // DK
    return pl.pallas_call(
        _cmp_prompt_kernel, grid=(2,),
        in_specs=[pl.BlockSpec((t, DK), lambda j: (0, base + j)),
                  pl.BlockSpec((1, CMP_BLOCK, DK), lambda j: (j, 0, 0)),
                  pl.BlockSpec((1, CMP_BLOCK * DK, hid), lambda j: (j, 0, 0)),
                  pl.BlockSpec((1, hid, DK), lambda j: (j, 0, 0))],
        out_specs=pl.BlockSpec((1, nb, DK), lambda j: (j, 0, 0)),
        out_shape=jax.ShapeDtypeStruct((2, nb, DK), BF16),
        scratch_shapes=[pltpu.VMEM((nb, CMP_BLOCK * DK), BF16)],
        compiler_params=_cp(("parallel",)), name="compress_prompt")(p, pos, w1, w2)


def _start_burst(copies):
    for i, cp in enumerate(copies):
        cp.start(priority=i % 2)


def _cmp_sample_kernel(pt_ref, pos_ref, w1_ref, w2_ref, cache_ref, kc_ref, vc_ref, buf, sem, x_scr, *, layer, pps):
    s = pl.program_id(0)
    n = pl.num_programs(0)
    slot = s % 2
    per_page = cache_ref.shape[2] // CMP_BLOCK
    fetch = lambda step, sl: [pltpu.make_async_copy(
        cache_ref.at[layer, pt_ref[step * pps + pg], pl.ds(blk * CMP_BLOCK, CMP_BLOCK), j, :],
        buf.at[sl, j, :, pg * per_page + blk, :], sem.at[sl])
        for j in range(2) for pg in range(pps) for blk in range(per_page)]

    @pl.when(s == 0)
    def _():
        _start_burst(fetch(0, 0))

    @pl.when(s + 1 < n)
    def _():
        _start_burst(fetch(s + 1, 1 - slot))

    for cp in fetch(s, slot):
        cp.wait()
    kc_ref[...] = _compress_rows(buf.at[slot, 0], x_scr, pos_ref, w1_ref, w2_ref, 0).astype(kc_ref.dtype)
    vc_ref[...] = _compress_rows(buf.at[slot, 1], x_scr, pos_ref, w1_ref, w2_ref, 1).astype(vc_ref.dtype)


def _compress_sample(pt_flat, cache_nsa, layer, pos, w1, w2, pps):
    page = cache_nsa.shape[2]
    n_tot = pt_flat.shape[0]
    nb = pps * page // CMP_BLOCK
    hid = w1.shape[-1]
    full = lambda shape: pl.BlockSpec(shape, lambda s, pt: (0,) * len(shape))
    grid_spec = pltpu.PrefetchScalarGridSpec(
        num_scalar_prefetch=1, grid=(n_tot // pps,),
        in_specs=[full((2, CMP_BLOCK, DK)), full((2, CMP_BLOCK * DK, hid)), full((2, hid, DK)),
                  pl.BlockSpec(memory_space=pl.ANY)],
        out_specs=[pl.BlockSpec((nb, DK), lambda s, pt: (s, 0)), pl.BlockSpec((nb, DK), lambda s, pt: (s, 0))],
        scratch_shapes=[pltpu.VMEM((2, 2, CMP_BLOCK, nb, DK), F32), pltpu.SemaphoreType.DMA((2,)),
                        pltpu.VMEM((nb, CMP_BLOCK * DK), BF16)])
    n_blocks = n_tot * page // CMP_BLOCK
    return pl.pallas_call(
        functools.partial(_cmp_sample_kernel, layer=layer, pps=pps), grid_spec=grid_spec,
        out_shape=[jax.ShapeDtypeStruct((n_blocks, DK), BF16), jax.ShapeDtypeStruct((n_blocks, DK), BF16)],
        compiler_params=_cp(("arbitrary",)), name="compress_sample")(pt_flat, pos, w1, w2, cache_nsa)


def _select_blocks(score, blk):
    sel = jnp.zeros(score.shape, F32)
    picks = []
    for _ in range(N_SELECT):
        top = jnp.max(score, axis=-1, keepdims=True)
        pick = jnp.min(jnp.where(score == top, blk, jnp.int32(1 << 30)), axis=-1, keepdims=True)
        hit = blk == pick
        sel = jnp.where(hit & (top > NEG / 2), 1.0, sel)
        score = jnp.where(hit, -3e38, score)
        picks.append(pick)
    return sel, picks


def _block_scores(imp, blk, cur):
    forced = (blk == 0) | (blk == cur) | (blk == cur - 1)
    return jnp.where(blk > cur, NEG, jnp.where(forced, FORCE_SCORE, imp))


def _cmp_attn_prompt_kernel(q_ref, kc_ref, vc_ref, near_ref, far_ref, o_ref, sel_ref):
    qt = pl.program_id(0)
    nbp = kc_ref.shape[0]
    q = jnp.concatenate([q_ref[:, h * DK:(h + 1) * DK].astype(BF16) for h in range(HEADS)], axis=0)
    s = _dot_t(q, kc_ref[...]) * NSA_SCALE
    lane = lax.broadcasted_iota(I32, (ROWS, nbp), 1)
    qpos = qt * TQ + (lax.broadcasted_iota(I32, (ROWS, nbp), 0) & (TQ - 1))
    bias = jnp.broadcast_to(far_ref[:, 0:1], (ROWS, nbp))
    for i in range(4):
        bias = jnp.where(lane == 2 * qt + (i - 2), near_ref[:, i:i + 1], bias)
    valid = CMP_BLOCK * lane + (CMP_BLOCK - 1) <= qpos
    s = jnp.where(valid, s + bias, NEG)
    e = jnp.where(valid, jnp.exp(s - jnp.max(s, axis=-1, keepdims=True)), 0.0)
    den = jnp.sum(e, axis=-1, keepdims=True)
    p = e / jnp.where(den > 0.0, den, 1.0)
    o = _dot(p.astype(BF16), vc_ref[...])
    imp = p[0:TQ]
    for h in range(1, HEADS):
        imp = imp + p[h * TQ:(h + 1) * TQ]
    for h in range(HEADS):
        o_ref[:, h * DK:(h + 1) * DK] = o[h * TQ:(h + 1) * TQ]
    blk = lax.broadcasted_iota(I32, (TQ, nbp), 1)
    cur = (qt * TQ + lax.broadcasted_iota(I32, (TQ, nbp), 0)) // CMP_BLOCK
    sel, _ = _select_blocks(_block_scores(imp, blk, cur), blk)
    sel_ref[...] = sel


def _cmp_attn_prompt(p, lay, kc, vc, near, far):
    t = p.shape[0]
    nbp = kc.shape[0]
    full = lambda a: pl.BlockSpec(a.shape, lambda i: (0,) * a.ndim)
    return pl.pallas_call(
        _cmp_attn_prompt_kernel, grid=(t // TQ,),
        in_specs=[pl.BlockSpec((TQ, NSA_W), functools.partial(lambda i, o: (i, o), o=lay["nsa_q"] // NSA_W)),
                  full(kc), full(vc), full(near), full(far)],
        out_specs=[pl.BlockSpec((TQ, NSA_W), lambda i: (i, 0)), pl.BlockSpec((TQ, nbp), lambda i: (i, 0))],
        out_shape=[jax.ShapeDtypeStruct((t, NSA_W), F32), jax.ShapeDtypeStruct((t, nbp), F32)],
        compiler_params=_cp(("parallel",)), name="cmp_attn_prompt")(p, kc, vc, near, far)


def _attn_prompt_kernel(*refs, dq, scale, mode):
    if mode == "sel":
        q_ref, k_ref, vt_ref, tab_ref, sel_ref, o_ref, q_scr, m_scr, l_scr, acc_scr = refs
    else:
        q_ref, k_ref, vt_ref, tab_ref, o_ref, q_scr, m_scr, l_scr, acc_scr = refs
    qt = pl.program_id(0)
    for h in range(HEADS):
        qh = q_ref[:, h * dq:(h + 1) * dq].astype(BF16)
        if mode == "sel":
            qh = jnp.concatenate([qh, ((sel_ref[...] - 1.0) * -NEG).astype(BF16)], axis=1)
        q_scr[h * TQ:(h + 1) * TQ, :] = qh
    m_scr[...] = jnp.full(m_scr.shape, NEG, F32)
    l_scr[...] = jnp.zeros(l_scr.shape, F32)
    acc_scr[...] = jnp.zeros(acc_scr.shape, F32)
    scale2 = scale * LOG2E

    def logits_of(kt):
        row = pl.multiple_of(kt * TQ, TQ)
        k = k_ref[pl.ds(row, TQ), :]
        if mode == "sel":
            nbp = sel_ref.shape[1]
            key = lax.broadcasted_iota(I32, (TQ, nbp), 0)
            blk = lax.broadcasted_iota(I32, (TQ, nbp), 1)
            expand = jnp.where(blk == 2 * kt + key // CMP_BLOCK, 1.0, 0.0).astype(BF16)
            k = jnp.concatenate([k, expand], axis=1)
        return _dot_t(k, q_scr[...])

    def softmax_update(logits, tab):
        weights, alphas = [], []
        for h in range(HEADS):
            s = logits[:, h * TQ:(h + 1) * TQ] * scale2
            if tab is not None:
                s = s + tab_ref[tab, h]
            m_prev = m_scr[h:h + 1, :]
            m_new = jnp.maximum(m_prev, jnp.max(s, axis=0, keepdims=True))
            alpha = jnp.exp2(m_prev - m_new)
            e = jnp.exp2(s - m_new)
            l_scr[h:h + 1, :] = alpha * l_scr[h:h + 1, :] + jnp.sum(e, axis=0, keepdims=True)
            m_scr[h:h + 1, :] = m_new
            weights.append(e.astype(BF16))
            alphas.append(alpha)
        return jnp.concatenate(weights, axis=1), jnp.concatenate(alphas, axis=1)

    def accumulate(kt, weights, alpha):
        acc_scr[...] = alpha * acc_scr[...] + _dot(vt_ref[kt], weights)

    def group(tiles):
        logits = [logits_of(kt) for kt in tiles]
        parts = [softmax_update(lg, None) for lg in logits]
        pvs = [_dot(vt_ref[kt], w) for kt, (w, _) in zip(tiles, parts)]
        acc = acc_scr[...]
        for (_, a), pv in zip(parts, pvs):
            acc = a * acc + pv
        acc_scr[...] = acc

    def stream(lo, hi):
        n_groups = (hi - lo) // UNROLL

        def grouped(i, carry):
            group([lo + i * UNROLL + u for u in range(UNROLL)])
            return carry

        def single(kt, carry):
            group([kt])
            return carry

        lax.fori_loop(0, n_groups, grouped, 0)
        lax.fori_loop(lo + n_groups * UNROLL, hi, single, 0)

    def special(delta, tab):
        @pl.when(qt >= delta)
        def _():
            w, a = softmax_update(logits_of(qt - delta), tab)
            accumulate(qt - delta, w, a)

    n_win = WINDOW // TQ
    if mode == "mla":
        stream(0, qt)
        special(0, 0)
    elif mode == "sel":
        stream(0, jnp.maximum(qt - 1, 0))
        special(1, 0)
        special(0, 1)
    else:
        special(n_win, 0)
        stream(jnp.maximum(qt - n_win + 1, 0), jnp.maximum(qt - 1, 0))
        special(1, 1)
        special(0, 2)
    for h in range(HEADS):
        o_ref[:, h * DK:(h + 1) * DK] = (acc_scr[:, h * TQ:(h + 1) * TQ] / l_scr[h:h + 1, :]).T


def _key_major(v):
    return jnp.transpose(v.reshape(v.shape[0] // TQ, TQ, DK), (0, 2, 1))


def _attn_prompt(q, q_col, dq, k, k_col, vt, tab, scale, mode, sel=None):
    t = k.shape[0]
    full = lambda a: pl.BlockSpec(a.shape, lambda i: (0,) * a.ndim)
    in_specs = [pl.BlockSpec((TQ, HEADS * dq), functools.partial(lambda i, o: (i, o), o=q_col)),
                pl.BlockSpec((t, dq), functools.partial(lambda i, o: (0, o), o=k_col)),
                full(vt), full(tab)]
    args = [q, k, vt, tab]
    dqa = dq
    if mode == "sel":
        in_specs.append(pl.BlockSpec((TQ, sel.shape[1]), lambda i: (i, 0)))
        args.append(sel)
        dqa = dq + sel.shape[1]
    return pl.pallas_call(
        functools.partial(_attn_prompt_kernel, dq=dq, scale=scale, mode=mode), grid=(t // TQ,),
        in_specs=in_specs,
        out_specs=pl.BlockSpec((TQ, NSA_W), lambda i: (i, 0)),
        out_shape=jax.ShapeDtypeStruct((t, NSA_W), F32),
        scratch_shapes=[pltpu.VMEM((ROWS, dqa), BF16), pltpu.VMEM((HEADS, TQ), F32),
                        pltpu.VMEM((HEADS, TQ), F32), pltpu.VMEM((DK, ROWS), F32)],
        compiler_params=_cp(("parallel",)), name="attn_prompt_" + mode)(*args)


def _softmax_rows(s):
    e = jnp.exp(s - jnp.max(s, axis=-1, keepdims=True))
    return e / jnp.sum(e, axis=-1, keepdims=True)


def _cmp_attn_sample_kernel(q_ref, kc_ref, vc_ref, tab_ref, o_ref, idx_ref, imp_scr, *, nbs):
    gb, nq, width = imp_scr.shape
    nbp = kc_ref.shape[0] // gb
    imp_scr[...] = jnp.zeros(imp_scr.shape, F32)
    for g in range(gb):
        q = q_ref[g * nq:(g + 1) * nq].reshape(nq * HEADS, DK).astype(BF16)
        kc = kc_ref[g * nbp:(g + 1) * nbp, :]
        p = _softmax_rows(_dot_t(q, kc) * NSA_SCALE + tab_ref[...])
        o_ref[g * nq:(g + 1) * nq] = _dot(p.astype(BF16), vc_ref[g * nbp:(g + 1) * nbp, :]).reshape(nq, HEADS, DK)
        imp_scr[g, :, 0:nbp] = jnp.sum(p.reshape(nq, HEADS, nbp), axis=1)
    blk = lax.broadcasted_iota(I32, imp_scr.shape, 2)
    imp = jnp.where(blk < nbs, imp_scr[...], 0.0)
    _, picks = _select_blocks(_block_scores(imp, blk, nbs), blk)
    lane = lax.broadcasted_iota(I32, idx_ref.shape, 2)
    out = jnp.zeros(idx_ref.shape, I32)
    for i, pick in enumerate(picks):
        out = jnp.where(lane == i, pick, out)
    idx_ref[...] = out


def _cmp_attn_sample(q3, kc, vc, tab, nq, nbs):
    ms = q3.shape[0]
    b = ms // nq
    nbp = kc.shape[0] // b
    gb = math.gcd(b, 8)
    return pl.pallas_call(
        functools.partial(_cmp_attn_sample_kernel, nbs=nbs), grid=(b // gb,),
        in_specs=[pl.BlockSpec((gb * nq, HEADS, DK), lambda i: (i, 0, 0)),
                  pl.BlockSpec((gb * nbp, DK), lambda i: (i, 0)), pl.BlockSpec((gb * nbp, DK), lambda i: (i, 0)),
                  pl.BlockSpec(tab.shape, lambda i: (0, 0))],
        out_specs=[pl.BlockSpec((gb * nq, HEADS, DK), lambda i: (i, 0, 0)),
                   pl.BlockSpec((gb, nq, LANE), lambda i: (i, 0, 0))],
        out_shape=[jax.ShapeDtypeStruct((ms, HEADS, DK), F32), jax.ShapeDtypeStruct((b, nq, LANE), I32)],
        scratch_shapes=[pltpu.VMEM((gb, nq, _align(nbs + 1, LANE)), F32)],
        compiler_params=_cp(("parallel",)), name="cmp_attn_sample")(q3, kc, vc, tab)


def _pad_new_rows(pad_scr, new):
    pad_scr[...] = jnp.zeros(pad_scr.shape, F32)
    pad_scr[0:new.shape[0], :] = new
    return pad_scr[...]


def _win_sample_kernel(q_ref, st_ref, new_ref, tab_ref, o_ref, pad_scr):
    nq = q_ref.shape[0]
    q = q_ref[...].reshape(nq * HEADS, DK).astype(BF16)
    st = st_ref.at[0]
    k_st = st[pl.ds(0, WINDOW, stride=2), :].astype(BF16)
    v_st = st[pl.ds(1, WINDOW, stride=2), :].astype(BF16)
    newp = _pad_new_rows(pad_scr, new_ref[0])
    s = jnp.concatenate([_dot_t(q, k_st), _dot_t(q, newp[:, :DK].astype(BF16))], axis=1)
    p = _softmax_rows(s * NSA_SCALE + tab_ref[...]).astype(BF16)
    o = _dot(p[:, :WINDOW], v_st) + _dot(p[:, WINDOW:], newp[:, DK:].astype(BF16))
    o_ref[...] = o.reshape(nq, HEADS, DK)


def _win_sample(q3, state, new, tab, nq):
    ms = q3.shape[0]
    return pl.pallas_call(
        _win_sample_kernel, grid=(ms // nq,),
        in_specs=[pl.BlockSpec((nq, HEADS, DK), lambda i: (i, 0, 0)),
                  pl.BlockSpec((1, 2 * WINDOW, DK), lambda i: (i, 0, 0)),
                  pl.BlockSpec((1, nq, 2 * DK), lambda i: (i, 0, 0)),
                  pl.BlockSpec(tab.shape, lambda i: (0, 0))],
        out_specs=pl.BlockSpec((nq, HEADS, DK), lambda i: (i, 0, 0)),
        out_shape=jax.ShapeDtypeStruct((ms, HEADS, DK), F32),
        scratch_shapes=[pltpu.VMEM((LANE, 2 * DK), F32)],
        compiler_params=_cp(("parallel",)), name="win_sample")(q3, state, new, tab)


def _sel_copies(cache_ref, layer, idx_ref, pt_ref, step, nq, n_pages, nbs, buf, slot, sem):
    per_page = cache_ref.shape[2] // CMP_BLOCK
    copies = []
    for i in range(nq * N_SELECT):
        j = jnp.minimum(idx_ref[step * nq * N_SELECT + i], nbs - 1)
        page = pt_ref[step * n_pages + j // per_page]
        r0 = pl.multiple_of((j % per_page) * CMP_BLOCK, CMP_BLOCK)
        for plane in range(2):
            src = cache_ref.at[layer, page, pl.ds(r0, CMP_BLOCK), 2 + plane, :]
            dst = buf.at[slot, plane, pl.ds(i * CMP_BLOCK, CMP_BLOCK), :]
            copies.append(pltpu.make_async_copy(src, dst, sem.at[slot]))
    return copies


def _sel_sample_kernel(idx_ref, pt_ref, q_ref, new_ref, tab_ref, cache_ref, o_ref, buf, sem, pad_scr,
                       *, layer, n_pages, nbs):
    s = pl.program_id(0)
    n = pl.num_programs(0)
    nq = q_ref.shape[0]
    slot = s % 2
    fetch = lambda step, sl: _sel_copies(cache_ref, layer, idx_ref, pt_ref, step, nq, n_pages, nbs, buf, sl, sem)

    @pl.when(s == 0)
    def _():
        _start_burst(fetch(0, 0))

    @pl.when(s + 1 < n)
    def _():
        _start_burst(fetch(s + 1, 1 - slot))

    for cp in fetch(s, slot):
        cp.wait()
    newp = _pad_new_rows(pad_scr, new_ref[0])
    kn, vn = newp[:, :DK].astype(BF16), newp[:, DK:].astype(BF16)
    lane = lax.broadcasted_iota(I32, (HEADS, LANE), 1)
    span = N_SELECT * CMP_BLOCK
    for qi in range(nq):
        q = q_ref[qi].astype(BF16)
        k = buf[slot, 0, qi * span:(qi + 1) * span, :].astype(BF16)
        v = buf[slot, 1, qi * span:(qi + 1) * span, :].astype(BF16)

        def block_bias(i):
            j = idx_ref[(s * nq + qi) * N_SELECT + i]
            near = jnp.where(j == nbs - 1, tab_ref[0, qi], jnp.where(j == nbs - 2, tab_ref[1, qi], tab_ref[3, qi]))
            return jnp.where(j >= nbs, NEG, near)

        bias = jnp.concatenate([jnp.where(lane < CMP_BLOCK, block_bias(2 * g), block_bias(2 * g + 1))
                                for g in range(N_SELECT // 2)], axis=1)
        sc = jnp.concatenate([_dot_t(q, k) * NSA_SCALE + bias, _dot_t(q, kn) * NSA_SCALE + tab_ref[2, qi]], axis=1)
        p = _softmax_rows(sc).astype(BF16)
        o_ref[qi] = _dot(p[:, :span], v) + _dot(p[:, span:], vn)


def _sel_sample(idx_flat, pt_flat, q3, new, tab, cache_nsa, layer, nq, n_pages, nbs):
    ms = q3.shape[0]
    grid_spec = pltpu.PrefetchScalarGridSpec(
        num_scalar_prefetch=2, grid=(ms // nq,),
        in_specs=[pl.BlockSpec((nq, HEADS, DK), lambda i, a, b: (i, 0, 0)),
                  pl.BlockSpec((1, nq, 2 * DK), lambda i, a, b: (i, 0, 0)),
                  pl.BlockSpec(tab.shape, lambda i, a, b: (0, 0, 0, 0)),
                  pl.BlockSpec(memory_space=pl.ANY)],
        out_specs=pl.BlockSpec((nq, HEADS, DK), lambda i, a, b: (i, 0, 0)),
        scratch_shapes=[pltpu.VMEM((2, 2, nq * N_SELECT * CMP_BLOCK, DK), F32), pltpu.SemaphoreType.DMA((2,)),
                        pltpu.VMEM((LANE, 2 * DK), F32)])
    return pl.pallas_call(
        functools.partial(_sel_sample_kernel, layer=layer, n_pages=n_pages, nbs=nbs), grid_spec=grid_spec,
        out_shape=jax.ShapeDtypeStruct((ms, HEADS, DK), F32),
        compiler_params=_cp(("arbitrary",)), name="sel_sample")(idx_flat, pt_flat, q3, new, tab, cache_nsa)


def _mla_sample_kernel(pt_ref, q_ref, new_ref, tab_ref, cache_ref, o_ref, buf, sem, pad_scr, m_scr, l_scr, acc_scr,
                       *, layer, pps, n_chunk, group):
    s = pl.program_id(0)
    n = pl.num_programs(0)
    nq = q_ref.shape[0]
    slot = s % 2
    ch = s % n_chunk
    width = cache_ref.shape[2]
    fetch = lambda step, sl: [pltpu.make_async_copy(cache_ref.at[layer, pt_ref[step * pps + pg]], buf.at[sl, pg],
                                                    sem.at[sl]) for pg in range(pps)]

    @pl.when(s == 0)
    def _():
        for cp in fetch(0, 0):
            cp.start()

    @pl.when(s + 1 < n)
    def _():
        for cp in fetch(s + 1, 1 - slot):
            cp.start()

    for cp in fetch(s, slot):
        cp.wait()

    @pl.when(ch == 0)
    def _():
        m_scr[...] = jnp.full(m_scr.shape, NEG, F32)
        l_scr[...] = jnp.zeros(l_scr.shape, F32)
        acc_scr[...] = jnp.zeros(acc_scr.shape, F32)

    q = q_ref[...].reshape(nq * HEADS, QPAD).astype(BF16)
    q_feat = q[:, :width]
    scale2 = MLA_SCALE * LOG2E

    def update(sc, weighted_values):
        m_prev = m_scr[...]
        m_new = jnp.maximum(m_prev, jnp.max(sc, axis=-1, keepdims=True))
        alpha = jnp.exp2(m_prev - m_new)
        e = jnp.exp2(sc - m_new)
        l_scr[...] = alpha * l_scr[...] + jnp.sum(e, axis=-1, keepdims=True)
        acc_scr[...] = alpha * acc_scr[...] + weighted_values(e.astype(BF16))
        m_scr[...] = m_new

    def body(g, carry):
        pages = [buf[slot, g * group + i].astype(BF16) for i in range(group)]
        sc = jnp.concatenate([_dot(q_feat, pg) for pg in pages], axis=1) * scale2
        page_len = pages[0].shape[1]

        def weighted_values(e):
            out = _dot_t(e[:, 0:page_len], pages[0][:DK, :])
            for i in range(1, group):
                out = out + _dot_t(e[:, i * page_len:(i + 1) * page_len], pages[i][:DK, :])
            return out

        update(sc, weighted_values)
        return carry

    lax.fori_loop(0, pps // group, body, 0)

    @pl.when(ch == n_chunk - 1)
    def _():
        newp = _pad_new_rows(pad_scr, new_ref[0]).astype(BF16)
        update(_dot_t(q, newp) * scale2 + tab_ref[...], lambda e: _dot(e, newp[:, :DK]))
        o_ref[...] = (acc_scr[...] / l_scr[...]).reshape(nq, HEADS, DK)


def _mla_sample(pt_flat, q3, new, tab, cache_t, layer, nq, n_pages, pps):
    ms = q3.shape[0]
    b = ms // nq
    width, page = cache_t.shape[2], cache_t.shape[3]
    n_chunk = n_pages // pps
    group = pps
    grid_spec = pltpu.PrefetchScalarGridSpec(
        num_scalar_prefetch=1, grid=(b * n_chunk,),
        in_specs=[pl.BlockSpec((nq, HEADS, QPAD), lambda s, pt: (s // n_chunk, 0, 0)),
                  pl.BlockSpec((1, nq, QPAD), lambda s, pt: (s // n_chunk, 0, 0)),
                  pl.BlockSpec(tab.shape, lambda s, pt: (0, 0)),
                  pl.BlockSpec(memory_space=pl.ANY)],
        out_specs=pl.BlockSpec((nq, HEADS, DK), lambda s, pt: (s // n_chunk, 0, 0)),
        scratch_shapes=[pltpu.VMEM((2, pps, width, page), F32), pltpu.SemaphoreType.DMA((2,)),
                        pltpu.VMEM((LANE, QPAD), F32), pltpu.VMEM((nq * HEADS, 1), F32),
                        pltpu.VMEM((nq * HEADS, 1), F32), pltpu.VMEM((nq * HEADS, DK), F32)])
    return pl.pallas_call(
        functools.partial(_mla_sample_kernel, layer=layer, pps=pps, n_chunk=n_chunk, group=group),
        grid_spec=grid_spec, out_shape=jax.ShapeDtypeStruct((ms, HEADS, DK), F32),
        compiler_params=_cp(("arbitrary",)), name="mla_sample")(pt_flat, q3, new, tab, cache_t)


def _post_kernel(oc_ref, os_ref, ow_ref, g1_ref, nz_ref, mz_ref, lat_ref, wv_ref, v0_ref, v1_ref, v2_ref,
                 cb_ref, cz_ref, cw_ref, brc_ref, brn_ref, brm_ref):
    g = _sigmoid(g1_ref[...])
    for h in range(HEADS):
        hs = slice(h * DK, (h + 1) * DK)
        gate = lambda br: g[:, ROPE_DIM + br * HEADS + h:ROPE_DIM + br * HEADS + h + 1]
        o = gate(0) * oc_ref[:, hs] + gate(1) * os_ref[:, hs] + gate(2) * ow_ref[:, hs]
        brn_ref[:, hs] = (o * _silu(nz_ref[:, hs])).astype(BF16)
        mo = _dot(lat_ref[:, hs].astype(BF16), wv_ref[h])
        brm_ref[:, hs] = (mo * _silu(mz_ref[:, hs])).astype(BF16)
    y = cw_ref[0:1, :] * v0_ref[...] + cw_ref[1:2, :] * v1_ref[...] + cw_ref[2:3, :] * v2_ref[...]
    brc_ref[...] = (cb_ref[...] * y * _silu(cz_ref[...])).astype(BF16)


def _post_mix(o_c, o_s, o_w, p, lay, lat, wv, v0, v1, v2, conv_w, c):
    m = p.shape[0]
    tm = min(256, m)
    col = lambda name, w: pl.BlockSpec((tm, w), functools.partial(lambda i, o: (i, o), o=lay[name] // w))
    row = lambda w: pl.BlockSpec((tm, w), lambda i: (i, 0))
    return pl.pallas_call(
        _post_kernel, grid=(m // tm,),
        in_specs=[row(NSA_W), row(NSA_W), row(NSA_W), col("g1", LANE), col("nsa_z", NSA_W), col("mla_z", NSA_W),
                  row(NSA_W), pl.BlockSpec((HEADS, DK, DK), lambda i: (0, 0, 0)), row(c), row(c), row(c),
                  col("conv_b", c), col("conv_z", c), pl.BlockSpec(conv_w.shape, lambda i: (0, 0))],
        out_specs=[row(c), row(NSA_W), row(NSA_W)],
        out_shape=[jax.ShapeDtypeStruct((m, c), BF16), jax.ShapeDtypeStruct((m, NSA_W), BF16),
                   jax.ShapeDtypeStruct((m, NSA_W), BF16)],
        compiler_params=_cp(("parallel",)), name="post_mix",
    )(o_c, o_s, o_w, p, p, p, lat, wv, v0, v1, v2, p, p, conv_w)


def _merge_kernel(brc_ref, brn_ref, brm_ref, wc_ref, wn_ref, wm_ref, g0_ref, g1_ref, g2_ref, o_ref):
    merged = (_sigmoid(g0_ref[...]) * _dot(brc_ref[...], wc_ref[...])
              + _sigmoid(g1_ref[...]) * _dot(brn_ref[...], wn_ref[...])
              + _sigmoid(g2_ref[...]) * _dot(brm_ref[...], wm_ref[...]))
    o_ref[...] = merged.astype(BF16)


def _merge(brc, brn, brm, wc, wn, wm, p, lay, d):
    m = p.shape[0]
    tm = min(512, m)
    tn = _col_tile(d, 512)
    act = lambda a: pl.BlockSpec((tm, a.shape[1]), lambda j, i: (i, 0))
    wgt = lambda a: pl.BlockSpec((a.shape[0], tn), lambda j, i: (0, j))
    gate = lambda br: pl.BlockSpec((tm, tn), functools.partial(lambda j, i, o: (i, o + j), o=(lay["merge_g"] + br * d) // tn))
    return pl.pallas_call(
        _merge_kernel, grid=(d // tn, m // tm),
        in_specs=[act(brc), act(brn), act(brm), wgt(wc), wgt(wn), wgt(wm), gate(0), gate(1), gate(2)],
        out_specs=pl.BlockSpec((tm, tn), lambda j, i: (i, j)),
        out_shape=jax.ShapeDtypeStruct((m, d), BF16),
        compiler_params=_cp(("parallel", "parallel")), name="merge")(brc, brn, brm, wc, wn, wm, p, p, p)


def _project(x, lw):
    h = _rmsnorm(x, lw["norm_g"], BF16)
    p = _matmul(h, lw["w_in"], name="in_proj")
    conv_v, cqn, mrow, kpad, kvb = _pre_mix(p, lw["lay"], lw["c"], lw["ql"], lw["g_q"], lw["g_kv"], lw["cos"], lw["sin"])
    mq = _matmul(cqn, lw["w_uq"], tn_cap=1024, name="q_up_proj")
    return p, conv_v, mrow, kpad, kvb, mq


def _finish(x, p, lw, o_c, o_s, o_w, lat, v0, v1, v2):
    brc, brn, brm = _post_mix(o_c, o_s, o_w, p, lw["lay"], lat, lw["wv"], v0, v1, v2, lw["conv_w"], lw["c"])
    merged = _merge(brc, brn, brm, lw["wc"], lw["wn"], lw["wm"], p, lw["lay"], x.shape[1])
    return _matmul(merged, lw["w_out"], res=x, tn_cap=1024, name="out_proj")


def _prompt_layer(x, lw, tabs):
    lay, c = lw["lay"], lw["c"]
    t = x.shape[0]
    p, conv_v, mrow, kpad, kvb, mq = _project(x, lw)
    q_mla = _mla_query(mq, lw["wk"], lw["cos"], lw["sin"], BF16)
    zc = lambda n: jnp.zeros((n, c), F32)
    v1 = jnp.concatenate([zc(1), conv_v[:-1]], axis=0)
    v0 = jnp.concatenate([zc(2), conv_v[:-2]], axis=0)
    sel_tab, win_tab, mla_tab, cmp_near, far = tabs
    kcv = _compress_prompt(p, lay, lw["phi_pos"], lw["phi_w1"], lw["phi_w2"])
    nbp = _align(kcv.shape[1], LANE)
    kcv = jnp.pad(kcv, ((0, 0), (0, nbp - kcv.shape[1]), (0, 0)))
    o_c, sel = _cmp_attn_prompt(p, lay, kcv[0], kcv[1], cmp_near, far)
    q_col = lay["nsa_q"] // NSA_W
    o_s = _attn_prompt(p, q_col, DK, kvb, 2, _key_major(kvb[:, 3 * DK:4 * DK]), sel_tab, NSA_SCALE, "sel", sel)
    o_w = _attn_prompt(p, q_col, DK, kvb, 4, _key_major(kvb[:, 5 * DK:6 * DK]), win_tab, NSA_SCALE, "win")
    lat = _attn_prompt(q_mla, 0, QPAD, kpad, 0, _key_major(kpad[:, :DK]), mla_tab, MLA_SCALE, "mla")
    x_new = _finish(x, p, lw, o_c, o_s, o_w, lat, v0, v1, conv_v)
    kv0 = lay["nsa_kv"]
    wb = min(WINDOW, t)
    new_nsa = p[:, kv0:kv0 + 4 * DK].reshape(1, t, 4, DK)
    new_win = p[t - wb:, kv0 + 4 * DK:kv0 + 6 * DK].reshape(1, wb, 2, DK)
    return x_new, mrow[None], new_nsa, new_win, conv_v[None, -2:]


def _sample_layer(x, lw, tabs, layer, cache_mla_t, cache_nsa, state_win, state_conv, pt_flat, b, nq, n_pages):
    lay, c = lw["lay"], lw["c"]
    ms = x.shape[0]
    page = cache_nsa.shape[2]
    past = n_pages * page
    nbs = past // CMP_BLOCK
    cmp_tab, win_tab, sel_tab, mla_new = tabs
    p, conv_v, mrow, kpad, kvb, mq = _project(x, lw)
    q_mla = _mla_query(mq, lw["wk"], lw["cos"], lw["sin"], F32).reshape(ms, HEADS, QPAD)
    vp = jnp.concatenate([state_conv, conv_v.reshape(b, nq, c)], axis=1)
    v0, v1, v2 = (vp[:, i:i + nq].reshape(ms, c) for i in range(3))
    q3 = p[:, lay["nsa_q"]:lay["nsa_q"] + NSA_W].reshape(ms, HEADS, DK)
    kv0 = lay["nsa_kv"]
    kv_new = p[:, kv0:kv0 + KV_COLS].reshape(b, nq, 6, DK)
    kc, vc = _compress_sample(pt_flat, cache_nsa, layer, lw["phi_pos"], lw["phi_w1"], lw["phi_w2"], min(64, n_pages))
    nbp = cmp_tab.shape[1]
    padb = lambda a: jnp.pad(a.reshape(b, nbs, DK), ((0, 0), (0, nbp - nbs), (0, 0))).reshape(b * nbp, DK)
    o_c, idx = _cmp_attn_sample(q3, padb(kc), padb(vc), cmp_tab, nq, nbs)
    idx_flat = idx[:, :, :N_SELECT].reshape(-1)
    o_s = _sel_sample(idx_flat, pt_flat, q3, kv_new[:, :, 2:4].reshape(b, nq, 2 * DK), sel_tab, cache_nsa, layer,
                      nq, n_pages, nbs)
    win_new = kv_new[:, :, 4:6]
    o_w = _win_sample(q3, state_win.reshape(b, 2 * WINDOW, DK), win_new.reshape(b, nq, 2 * DK), win_tab, nq)
    mrow_pad = jnp.pad(mrow, ((0, 0), (0, QPAD - mrow.shape[1]))).reshape(b, nq, QPAD)
    lat = _mla_sample(pt_flat, q_mla, mrow_pad, mla_new, cache_mla_t, layer, nq, n_pages, min(64, n_pages))
    flat = lambda a: a.reshape(ms, NSA_W)
    x_new = _finish(x, p, lw, flat(o_c), flat(o_s), flat(o_w), flat(lat), v0, v1, v2)
    new_win_state = jnp.concatenate([state_win, win_new], axis=1)[:, -WINDOW:]
    return (x_new, mrow.reshape(b, nq, -1), kv_new[:, :, 0:4], new_win_state, vp[:, -2:])


def kernel(x_prompt, x_sample, cache_mla, cache_nsa, state_nsa_win, state_conv, page_table, norm_g, w_in, conv_w,
           phi_pos, phi_w1, phi_w2, mla_q_norm, mla_kv_norm, w_uq, w_ukv, rel_bias, w_branch, w_out, final_g):
    depth = w_in.shape[0]
    bp, t, d = x_prompt.shape
    b, nq, _ = x_sample.shape
    c = conv_w.shape[-1]
    ql = mla_q_norm.shape[-1]
    hid = phi_w2.shape[2]
    n_pages = page_table.shape[1]
    page = cache_mla.shape[2]
    past = n_pages * page
    assert bp == 1 and t % TQ == 0 and t >= WINDOW and WINDOW % TQ == 0
    assert state_nsa_win.shape[2] == WINDOW and past >= WINDOW and past % CMP_BLOCK == 0
    assert (past + nq) // CMP_BLOCK == past // CMP_BLOCK and past // CMP_BLOCK >= N_SELECT
    assert page % CMP_BLOCK == 0 and nq <= 8 and w_ukv.shape[1:] == (DK, HEADS, 2 * DK)

    lay, n_pack = _pack_layout(c, ql, d)
    tbl = rel_bias[jnp.asarray(_bucket_of_dist())].T.astype(F32)
    nbs = past // CMP_BLOCK
    tabs_p = _prompt_tables(tbl)
    tabs_s = _sample_tables(tbl, past, nq, _align(nbs, LANE))
    cos_p, sin_p = _rope_tables(np.arange(t))
    cos_s, sin_s = _rope_tables(np.tile(past + np.arange(nq), b))
    cache_mla_t = jnp.swapaxes(cache_mla, 2, 3)
    pt_flat = page_table.reshape(-1)

    xp = x_prompt.reshape(t, d)
    xs = x_sample.reshape(b * nq, d)
    outs = [[] for _ in range(8)]
    for l in range(depth):
        lw = {
            "lay": lay, "c": c, "ql": ql, "norm_g": norm_g[l],
            "w_in": _pack_w_in(w_in[l], lay, n_pack, c, ql, d),
            "g_q": mla_q_norm[l], "g_kv": mla_kv_norm[l], "w_uq": _pack_w_uq(w_uq[l]),
            "wk": jnp.transpose(w_ukv[l][:, :, :DK], (1, 2, 0)).astype(BF16),
            "wv": jnp.transpose(w_ukv[l][:, :, DK:], (1, 0, 2)).astype(BF16),
            "phi_pos": phi_pos[l], "phi_w1": phi_w1[l].reshape(2, CMP_BLOCK * DK, hid).astype(BF16),
            "phi_w2": phi_w2[l].astype(BF16), "conv_w": conv_w[l],
            "wc": w_branch[l][:c].astype(BF16), "wn": w_branch[l][c:c + NSA_W].astype(BF16),
            "wm": w_branch[l][c + NSA_W:].astype(BF16), "w_out": w_out[l].astype(BF16),
        }
        xp, mla_p, nsa_p, win_p, conv_p = _prompt_layer(xp, dict(lw, cos=cos_p, sin=sin_p), tabs_p)
        xs, mla_s, nsa_s, win_s, conv_s = _sample_layer(
            xs, dict(lw, cos=cos_s, sin=sin_s), tabs_s, l, cache_mla_t, cache_nsa, state_nsa_win[l], state_conv[l],
            pt_flat, b, nq, n_pages)
        for lst, val in zip(outs, (mla_p, mla_s, nsa_p, nsa_s, win_p, win_s, conv_p, conv_s)):
            lst.append(val)
    y_prompt = _rmsnorm(xp, final_g, F32).reshape(1, t, d)
    y_sample = _rmsnorm(xs, final_g, F32).reshape(b, nq, d)
    return (y_prompt, y_sample) + tuple(jnp.stack(o) for o in outs)
```
